```python
import jax, jax.numpy as jnp
from jax import lax
import numpy as np

D_MODEL = 1024
BATCH = 8
SEQ = 2048
DEPTH = 4

GRID_W = 64
CTX_LEN = 256
A_HEADS = 4
A_DK = 128
A_DV = 128
A_KW = A_HEADS * A_DK
A_VW = A_HEADS * A_DV
A_CHUNK = 16
B_WIDTH = 1024
B_HEADDIM = 64
B_HEADS = B_WIDTH // B_HEADDIM
B_GROUPS = 4
B_HPG = B_HEADS // B_GROUPS
B_STATE = 128
B_CONV = 5
B_CONV_CH = B_WIDTH + 2 * B_GROUPS * B_STATE
B_CHUNK = 128
IN_SIZES = (A_KW, A_KW, A_KW, A_VW, A_VW, B_WIDTH, B_CONV_CH, B_HEADS, B_HEADS, D_MODEL, D_MODEL)
IN_COLS = sum(IN_SIZES)
IN_SPLITS = tuple(int(s) for s in np.cumsum(IN_SIZES)[:-1])
DN_ALPHA = (2 * DEPTH) ** 0.25
DN_BETA = (8 * DEPTH) ** -0.25
LN_EPS = 1e-5
RMS_EPS = 1e-6
F_FLOOR = 1e-30

kernel_name = "hybrid_hgrn2_ssd_prefix_dit"


def layer_norm(x, g, b):
    xf = x.astype(jnp.float32)
    mu = jnp.mean(xf, -1, keepdims=True)
    var = jnp.mean(jnp.square(xf - mu), -1, keepdims=True)
    return (xf - mu) * lax.rsqrt(var + LN_EPS) * g + b


def group_rms(u, groups):
    u = u.astype(jnp.float32)
    sh = u.shape
    u = u.reshape(*sh[:-1], groups, sh[-1] // groups)
    u = u * lax.rsqrt(jnp.mean(u * u, -1, keepdims=True) + RMS_EPS)
    return u.reshape(sh)


def masked_exp(diff, mask):
    return jnp.where(mask, jnp.exp(jnp.where(mask, diff, 0.0)), 0.0)


def dwconv_centred(u, w, b, n_rows):
    bsz, L, C = u.shape
    v = u.reshape(bsz * n_rows, L // n_rows, C)
    pad = B_CONV // 2
    out = lax.conv_general_dilated(v, w[:, None, :].astype(v.dtype), (1,), [(pad, pad)],
                                   dimension_numbers=('NWC', 'WIO', 'NWC'), feature_group_count=C)
    return out.reshape(bsz, L, C) + b


def gla_chunked(q, k, v, logf, s0, need_out):
    bsz, L, H, dk = q.shape
    dv = v.shape[-1]
    n = L // A_CHUNK
    rs = lambda t: t.reshape(bsz, n, A_CHUNK, H, t.shape[-1])
    q, k, v = rs(q), rs(k), rs(v)
    b = jnp.cumsum(rs(logf).astype(jnp.float32), axis=2)
    b_last = b[:, :, -1]
    u = jnp.einsum('bnchk,bnchv->bnhkv', k * jnp.exp(b_last[:, :, None] - b), v).astype(jnp.float32)

    def step(s, inp):
        dec, un = inp
        return jnp.exp(dec)[..., None] * s + un, s

    s_fin, s_prev = lax.scan(step, s0.astype(jnp.float32),
                             (jnp.moveaxis(b_last, 1, 0), jnp.moveaxis(u, 1, 0)))
    if not need_out:
        return None, s_fin
    s_prev = jnp.moveaxis(s_prev, 0, 1)
    o_inter = jnp.einsum('bnthk,bnhkv->bnthv', q * jnp.exp(b), s_prev)
    causal = jnp.tril(jnp.ones((A_CHUNK, A_CHUNK), dtype=bool))[None, None, :, :, None, None]
    dec = masked_exp(b[:, :, :, None] - b[:, :, None, :], causal)
    scores = jnp.einsum('bnthk,bntshk,bnshk->bnths', q, dec, k)
    o = o_inter + jnp.einsum('bnths,bnshv->bnthv', scores, v)
    return o.reshape(bsz, L, H, dv), s_fin


def ssd_chunked(xh, dt, a_neg, bm, cm, s0, need_out):
    bsz, L = xh.shape[:2]
    n = L // B_CHUNK
    G, R, N, P = B_GROUPS, B_HPG, B_STATE, B_HEADDIM
    x_dt = (xh * dt[..., None]).reshape(bsz, n, B_CHUNK, G, R, P)
    cum = jnp.cumsum((dt * a_neg).reshape(bsz, n, B_CHUNK, G, R), axis=2)
    cum_last = cum[:, :, -1]
    bc = bm.reshape(bsz, n, B_CHUNK, G, N)
    cc = cm.reshape(bsz, n, B_CHUNK, G, N)
    u = jnp.einsum('bktgn,bktgr,bktgrp->bkgrnp', bc, jnp.exp(cum_last[:, :, None] - cum), x_dt).astype(jnp.float32)

    def step(s, inp):
        dec, un = inp
        return jnp.exp(dec)[..., None, None] * s + un, s

    s_fin, s_prev = lax.scan(step, s0.astype(jnp.float32),
                             (jnp.moveaxis(cum_last, 1, 0), jnp.moveaxis(u, 1, 0)))
    if not need_out:
        return None, s_fin
    s_prev = jnp.moveaxis(s_prev, 0, 1)
    y_inter = jnp.einsum('bktgn,bktgr,bkgrnp->bktgrp', cc, jnp.exp(cum), s_prev)
    cb = jnp.einsum('bktgn,bksgn->bktsg', cc, bc)
    causal = jnp.tril(jnp.ones((B_CHUNK, B_CHUNK), dtype=bool))[None, None, :, :, None, None]
    lmat = masked_exp(cum[:, :, :, None] - cum[:, :, None, :], causal)
    y_intra = jnp.einsum('bktsg,bktsgr,bksgrp->bktgrp', cb, lmat, x_dt)
    return (y_inter + y_intra).reshape(bsz, L, B_HEADS, P), s_fin


def hgrn2_branch(q, fz_f, fz_b, i, g, lb_f, lb_b, norm_w, s0_f, s0_b, need_out):
    bsz, L = q.shape[:2]
    hd = lambda t, d: t.reshape(bsz, L, A_HEADS, d)
    q, i = hd(q, A_DK), hd(i, A_DV)

    def direction(fz, lb, s0, flip):
        lb = lb.reshape(A_HEADS, A_DK)
        f = lb + (1.0 - lb) * jax.nn.sigmoid(hd(fz, A_DK).astype(jnp.float32))
        logf = jnp.log(jnp.maximum(f, F_FLOOR))
        k = 1.0 - f
        qq, kk, vv = q, k, i
        if flip:
            qq, kk, vv, logf = qq[:, ::-1], kk[:, ::-1], vv[:, ::-1], logf[:, ::-1]
        o, s = gla_chunked(qq, kk, vv, logf, s0, need_out)
        if flip and o is not None:
            o = o[:, ::-1]
        return o, s

    o_f, s_f = direction(fz_f, lb_f, s0_f, False)
    o_b, s_b = direction(fz_b, lb_b, s0_b, True)
    if not need_out:
        return None, s_f, s_b
    o = (o_f + o_b).reshape(bsz, L, A_VW)
    y = group_rms(o, A_HEADS) * norm_w.reshape(1, A_DV).repeat(A_HEADS, 0).reshape(A_VW) * jax.nn.silu(g)
    return y, s_f, s_b


def ssd_branch(z, xbc, dtr_f, dtr_b, conv_w, conv_b, dt_bias, a_log, d_skip, norm_w, s0_f, s0_b, n_rows, need_out):
    bsz, L = xbc.shape[:2]
    xbc = jax.nn.silu(dwconv_centred(xbc, conv_w, conv_b, n_rows))
    xs, bm, cm = jnp.split(xbc, (B_WIDTH, B_WIDTH + B_GROUPS * B_STATE), axis=-1)
    xh = xs.reshape(bsz, L, B_HEADS, B_HEADDIM)
    bm = bm.reshape(bsz, L, B_GROUPS, B_STATE)
    cm = cm.reshape(bsz, L, B_GROUPS, B_STATE)

    def direction(dtr, d, s0, flip):
        dt = jax.nn.softplus(dtr.astype(jnp.float32) + dt_bias[d])
        a_neg = -jnp.exp(a_log[d].astype(jnp.float32))
        xx, dd, bb, cc = xh, dt, bm, cm
        if flip:
            xx, dd, bb, cc = xx[:, ::-1], dd[:, ::-1], bb[:, ::-1], cc[:, ::-1]
        y, s = ssd_chunked(xx, dd, a_neg, bb, cc, s0, need_out)
        if flip and y is not None:
            y = y[:, ::-1]
        return y, s

    y_f, s_f = direction(dtr_f, 0, s0_f, False)
    y_b, s_b = direction(dtr_b, 1, s0_b, True)
    if not need_out:
        return None, s_f, s_b
    y = (y_f + y_b + d_skip[:, None] * xh).reshape(bsz, L, B_WIDTH)
    return group_rms(y * jax.nn.silu(z), B_GROUPS) * norm_w, s_f, s_b


def hybrid_mixer(h, w_in_l, lb_f, lb_b, a_norm_w_l, conv_w, conv_b, dt_bias, a_log, d_skip, b_norm_w_l,
                 wpa, wpb, wo, states0, n_rows, need_out):
    u = h @ w_in_l
    aq, af_f, af_b, ai, ag, bz, bxbc, bdt_f, bdt_b, gate_a, gate_b = jnp.split(u, IN_SPLITS, axis=-1)
    ya, sa_f, sa_b = hgrn2_branch(aq, af_f, af_b, ai, ag, lb_f, lb_b, a_norm_w_l,
                                  states0[0], states0[1], need_out)
    yb, sb_f, sb_b = ssd_branch(bz, bxbc, bdt_f, bdt_b, conv_w, conv_b, dt_bias, a_log, d_skip, b_norm_w_l,
                                states0[2], states0[3], n_rows, need_out)
    states = (sa_f, sa_b, sb_f, sb_b)
    if not need_out:
        return None, states
    merged = jax.nn.sigmoid(gate_a) * (ya @ wpa) + jax.nn.sigmoid(gate_b) * (yb @ wpb)
    return merged @ wo, states


def zero_states(bsz):
    sa = jnp.zeros((bsz, A_HEADS, A_DK, A_DV), jnp.float32)
    sb = jnp.zeros((bsz, B_GROUPS, B_HPG, B_STATE, B_HEADDIM), jnp.float32)
    return (sa, sa, sb, sb)


def setup_inputs(seed: int = 0) -> dict:
    key = jax.random.key(seed)
    ks = jax.random.split(key, 20)
    nrm = jax.random.normal
    D = D_MODEL
    x = nrm(ks[0], (BATCH, SEQ, D), jnp.float32)
    c = nrm(ks[1], (BATCH, D), jnp.float32)
    ctx = nrm(ks[2], (BATCH, CTX_LEN, D), jnp.float32)
    c_ctx = nrm(ks[3], (D,), jnp.float32)
    w_mod = nrm(ks[4], (DEPTH, D, 3 * D), jnp.float32) * (0.1 * D ** -0.5)
    gate_one = jnp.concatenate([jnp.zeros((2 * D,), jnp.float32), jnp.ones((D,), jnp.float32)])
    b_mod = 0.02 * nrm(ks[5], (DEPTH, 3 * D), jnp.float32) + gate_one
    w_in = nrm(ks[6], (DEPTH, D, IN_COLS), jnp.float32) * D ** -0.5
    a_lb_logits = 0.5 * nrm(ks[7], (2, DEPTH, A_KW), jnp.float32)
    a_norm_w = 1.0 + 0.02 * nrm(ks[8], (DEPTH, A_DV), jnp.float32)
    b_conv_w = nrm(ks[9], (DEPTH, B_CONV, B_CONV_CH), jnp.float32) * B_CONV ** -0.5
    b_conv_b = 0.02 * nrm(ks[10], (DEPTH, B_CONV_CH), jnp.float32)
    dt0 = jnp.exp(jax.random.uniform(ks[11], (DEPTH, 2, B_HEADS), jnp.float32, np.log(1e-3), np.log(1e-1)))
    b_dt_bias = dt0 + jnp.log(-jnp.expm1(-dt0))
    b_a_log = jnp.log(jax.random.uniform(ks[12], (DEPTH, 2, B_HEADS), jnp.float32, 1.0, 16.0))
    b_d = 1.0 + 0.1 * nrm(ks[13], (DEPTH, B_HEADS), jnp.float32)
    b_norm_w = 1.0 + 0.02 * nrm(ks[14], (DEPTH, B_WIDTH), jnp.float32)
    w_proj_a = nrm(ks[15], (DEPTH, A_VW, D), jnp.float32) * (A_VW ** -0.5 * DN_BETA)
    w_proj_b = nrm(ks[16], (DEPTH, B_WIDTH, D), jnp.float32) * (B_WIDTH ** -0.5 * DN_BETA)
    w_out = nrm(ks[17], (DEPTH, D, D), jnp.float32) * (D ** -0.5 * DN_BETA)
    ln_g = 1.0 + 0.02 * nrm(ks[18], (DEPTH, D), jnp.float32)
    ln_b = 0.02 * nrm(ks[19], (DEPTH, D), jnp.float32)
    return {'x': x, 'c': c, 'ctx': ctx, 'c_ctx': c_ctx, 'w_mod': w_mod, 'b_mod': b_mod, 'w_in': w_in,
            'a_lb_logits': a_lb_logits, 'a_norm_w': a_norm_w, 'b_conv_w': b_conv_w, 'b_conv_b': b_conv_b,
            'b_dt_bias': b_dt_bias, 'b_a_log': b_a_log, 'b_d': b_d, 'b_norm_w': b_norm_w,
            'w_proj_a': w_proj_a, 'w_proj_b': w_proj_b, 'w_out': w_out, 'ln_g': ln_g, 'ln_b': ln_b}


def reference(x, c, ctx, c_ctx, w_mod, b_mod, w_in, a_lb_logits, a_norm_w, b_conv_w, b_conv_b,
              b_dt_bias, b_a_log, b_d, b_norm_w, w_proj_a, w_proj_b, w_out, ln_g, ln_b):
    D = D_MODEL
    bsz, L, _ = x.shape
    rows = L // GRID_W
    sm = jax.nn.softmax(a_lb_logits.astype(jnp.float32), axis=1)
    lower = jnp.cumsum(sm, axis=1) - sm[:, :1]
    c_act = jax.nn.silu(c)
    cc_act = jax.nn.silu(c_ctx)
    xl, xc = x, ctx
    for l in range(DEPTH):
        last = l == DEPTH - 1
        ml = c_act @ w_mod[l] + b_mod[l]
        mc = cc_act @ w_mod[l] + b_mod[l]
        shift_l, scale_l, gate_l = ml[:, None, :D], ml[:, None, D:2 * D], ml[:, None, 2 * D:]
        shift_c, scale_c, gate_c = mc[:D], mc[D:2 * D], mc[2 * D:]
        common = (w_in[l], lower[0, l], lower[1, l], a_norm_w[l], b_conv_w[l], b_conv_b[l], b_dt_bias[l],
                  b_a_log[l], b_d[l], b_norm_w[l], w_proj_a[l], w_proj_b[l], w_out[l])
        hc = xc * (1.0 + scale_c) + shift_c
        out_c, ctx_states = hybrid_mixer(hc, *common, zero_states(bsz), 1, not last)
        hl = xl * (1.0 + scale_l) + shift_l
        out_l, _ = hybrid_mixer(hl, *common, ctx_states, rows, True)
        xl = layer_norm(DN_ALPHA * xl + gate_l * out_l, ln_g[l], ln_b[l])
        if not last:
            xc = layer_norm(DN_ALPHA * xc + gate_c * out_c, ln_g[l], ln_b[l])
    return xl
```

```python
import functools

import numpy as np
import jax
import jax.numpy as jnp
from jax import lax
from jax.experimental import pallas as pl
from jax.experimental.pallas import tpu as pltpu

F32 = jnp.float32
BF16 = jnp.bfloat16

A_HEADS = 4
A_DK = 128
A_KW = A_HEADS * A_DK
B_WIDTH = 1024
B_HEADDIM = 64
B_HEADS = B_WIDTH // B_HEADDIM
B_GROUPS = 4
B_HPG = B_HEADS // B_GROUPS
B_STATE = 128
B_CONV = 5
GRID_W = 64
LN_EPS = 1e-5
RMS_EPS = 1e-6
F_FLOOR = 1e-30

CHUNK = 128
LANES = 128
VMEM_LIMIT = 56 * 1024 * 1024

COL_GATE = 0
COL_A = 2048
COL_Z = COL_A + 5 * A_KW
COL_XBC = COL_Z + B_WIDTH
N_U = COL_XBC + B_WIDTH + 2 * B_GROUPS * B_STATE


def _sigmoid(x):
    return 1.0 / (1.0 + jnp.exp(-x))


def _softplus(x):
    return jnp.maximum(x, 0.0) + jnp.log(1.0 + jnp.exp(-jnp.abs(x)))


def _split3(x):
    hi = x.astype(BF16)
    r = x - hi.astype(F32)
    mid = r.astype(BF16)
    lo = (r - mid.astype(F32)).astype(BF16)
    return hi, mid, lo


def _cumsum_rows(tri, x):
    hi, mid, lo = _split3(x)
    d = lambda a: jnp.dot(tri, a, preferred_element_type=F32)
    return d(hi) + d(mid) + d(lo)


def _dot_nt(a, b):
    return lax.dot_general(a, b, (((1,), (1,)), ((), ())), preferred_element_type=F32)


def _dot_tn(a, b):
    return lax.dot_general(a, b, (((0,), (0,)), ((), ())), preferred_element_type=F32)


def _cparams(sem):
    return pltpu.CompilerParams(dimension_semantics=sem, vmem_limit_bytes=VMEM_LIMIT)


def _mod_kernel(c_ref, w_ref, b_ref, o_ref):
    a = c_ref[...]
    a = a * _sigmoid(a)
    o_ref[0] = jnp.dot(a, w_ref[0], preferred_element_type=F32) + b_ref[0]


def _modulation(cc, w_mod, b_mod):
    depth, d, d3 = w_mod.shape
    rows = cc.shape[0]
    tn = 1024
    return pl.pallas_call(
        _mod_kernel,
        out_shape=jax.ShapeDtypeStruct((depth, rows, d3), F32),
        grid=(depth, d3 // tn),
        in_specs=[pl.BlockSpec((rows, d), lambda l, j: (0, 0)),
                  pl.BlockSpec((1, d, tn), lambda l, j: (l, 0, j)),
                  pl.BlockSpec((1, 1, tn), lambda l, j: (l, 0, j))],
        out_specs=pl.BlockSpec((1, rows, tn), lambda l, j: (l, 0, j)),
        compiler_params=_cparams(("arbitrary", "arbitrary")),
        name="modulation",
    )(cc, w_mod, b_mod.reshape(depth, 1, d3))


def _modulate(x, mod_ref, tok0, lc):
    tm = x.shape[0]
    tok = tok0 + lax.broadcasted_iota(jnp.int32, (tm, 1), 0)
    is_ctx = tok < lc
    shift = jnp.where(is_ctx, mod_ref[0, 1, 0:1, :], mod_ref[0, 0, 0:1, :])
    scale = jnp.where(is_ctx, mod_ref[0, 1, 1:2, :], mod_ref[0, 0, 1:2, :])
    return x * (1.0 + scale) + shift


def _inproj_kernel(x_ref, mod_ref, w_ref, cw_ref, cb_ref, o_ref, h_ref, *, lc, conv_lo, conv_hi):
    t = pl.program_id(1)
    j = pl.program_id(2)
    tm = x_ref.shape[1]

    @pl.when(j == 0)
    def _():
        h_ref[...] = _modulate(x_ref[0], mod_ref, t * tm, lc).astype(BF16)

    acc = jnp.dot(h_ref[...], w_ref[...], preferred_element_type=F32)
    is_conv = jnp.logical_and(j >= conv_lo, j < conv_hi)

    @pl.when(is_conv)
    def _():
        tok = t * tm + lax.broadcasted_iota(jnp.int32, (tm, 1), 0)
        is_ctx = tok < lc
        rowlen = jnp.where(is_ctx, lc, GRID_W)
        pos = jnp.where(is_ctx, tok, lax.rem(tok - lc, GRID_W))
        out = jnp.zeros_like(acc) + cb_ref[...]
        pad = B_CONV // 2
        for tap in range(B_CONV):
            d = tap - pad
            v = acc if d == 0 else pltpu.roll(acc, (-d) % tm, axis=0)
            valid = jnp.logical_and(pos + d >= 0, pos + d < rowlen)
            out = out + jnp.where(valid, v, 0.0) * cw_ref[tap:tap + 1, :]
        o_ref[0] = (out * _sigmoid(out)).astype(o_ref.dtype)

    @pl.when(jnp.logical_not(is_conv))
    def _():
        o_ref[0] = acc.astype(o_ref.dtype)


def _pick_tile(total, candidates):
    for c in candidates:
        if total % c == 0:
            return c
    raise ValueError(f"no tile for {total}")


def _inproj(xcat, mod, w_u, conv_w, conv_b, lc):
    bsz, t_tot, d = xcat.shape
    n_u = w_u.shape[1]
    tm = _pick_tile(t_tot, (768, 512, 256))
    tn = 512
    conv_lo, conv_hi = COL_XBC // tn, n_u // tn
    n_conv = conv_hi - conv_lo
    cmap = lambda b, t, j: (0, jnp.clip(j - conv_lo, 0, n_conv - 1))
    kern = functools.partial(_inproj_kernel, lc=lc, conv_lo=conv_lo, conv_hi=conv_hi)
    return pl.pallas_call(
        kern,
        out_shape=jax.ShapeDtypeStruct((bsz, t_tot, n_u), BF16),
        grid=(bsz, t_tot // tm, n_u // tn),
        in_specs=[pl.BlockSpec((1, tm, d), lambda b, t, j: (b, t, 0)),
                  pl.BlockSpec((1, 2, 3, d), lambda b, t, j: (b, 0, 0, 0)),
                  pl.BlockSpec((d, tn), lambda b, t, j: (0, j)),
                  pl.BlockSpec((B_CONV, tn), cmap),
                  pl.BlockSpec((1, tn), cmap)],
        out_specs=pl.BlockSpec((1, tm, tn), lambda b, t, j: (b, t, j)),
        scratch_shapes=[pltpu.VMEM((tm, d), BF16)],
        compiler_params=_cparams(("parallel", "parallel", "arbitrary")),
        name="inproj",
    )(xcat, mod, w_u, conv_w, conv_b)


def _dtproj_kernel(x_ref, mod_ref, w_ref, o_ref, *, lc):
    t = pl.program_id(1)
    tm = x_ref.shape[1]
    h = _modulate(x_ref[0], mod_ref, t * tm, lc).astype(BF16)
    o_ref[0] = jnp.dot(h, w_ref[...], preferred_element_type=F32)


def _dtproj(xcat, mod, w_dt, lc):
    bsz, t_tot, d = xcat.shape
    tm = _pick_tile(t_tot, (768, 512, 256))
    return pl.pallas_call(
        functools.partial(_dtproj_kernel, lc=lc),
        out_shape=jax.ShapeDtypeStruct((bsz, t_tot, LANES), F32),
        grid=(bsz, t_tot // tm),
        in_specs=[pl.BlockSpec((1, tm, d), lambda b, t: (b, t, 0)),
                  pl.BlockSpec((1, 2, 3, d), lambda b, t: (b, 0, 0, 0)),
                  pl.BlockSpec((d, LANES), lambda b, t: (0, 0))],
        out_specs=pl.BlockSpec((1, tm, LANES), lambda b, t: (b, t, 0)),
        compiler_params=_cparams(("parallel", "parallel")),
        name="dtproj",
    )(xcat, mod, w_dt)


def _level_table():
    t = np.arange(CHUNK)[:, None]
    s = np.arange(CHUNK)[None, :]
    x = t ^ s
    lvl = np.where(x > 0, 2 ** np.floor(np.log2(np.maximum(x, 1))).astype(np.int64), 0)
    return np.where(s <= t, lvl, -1).astype(np.int32)


def _boundary_rows(b, m, backward):
    n = b.shape[0]
    off = m if backward else m - 1
    if m >= 8:
        parts = [jnp.broadcast_to(b[g0 + off:g0 + off + 1, :], (2 * m, b.shape[1])) for g0 in range(0, n, 2 * m)]
        return parts[0] if len(parts) == 1 else jnp.concatenate(parts, axis=0)
    b3 = b.reshape(n // 8, 8, b.shape[1])
    sub = lax.broadcasted_iota(jnp.int32, b3.shape, 1)
    out = None
    for g0 in range(0, 8, 2 * m):
        src = jnp.broadcast_to(b3[:, g0 + off:g0 + off + 1, :], b3.shape)
        out = src if out is None else jnp.where(sub >= g0, src, out)
    return out.reshape(b.shape)


def _gla_chunk(q, z, v, lb, st, tri, lvl, backward):
    f = lb + (1.0 - lb) * _sigmoid(z)
    f = jnp.maximum(f, F_FLOOR)
    k = 1.0 - (lb + (1.0 - lb) * _sigmoid(z))
    logf = jnp.log(f)
    b = _cumsum_rows(tri, logf)
    row = lax.broadcasted_iota(jnp.int32, b.shape, 0)

    scores = jnp.where(lvl == 0, _dot_nt(q.astype(BF16), k.astype(BF16)), 0.0)
    m = 1
    while m < CHUNK:
        bit = (row & m) != 0
        qside = jnp.logical_not(bit) if backward else bit
        if m == 1:
            e = jnp.where(qside, f, 1.0)
        else:
            bm = _boundary_rows(b, m, backward)
            e = jnp.exp(jnp.where(qside, b - bm, bm - b))
        w = (jnp.where(qside, q, k) * e).astype(BF16)
        scores = jnp.where(lvl == m, _dot_nt(w, w), scores)
        m *= 2

    b_end = b[0:1, :] if backward else b[CHUNK - 1:CHUNK, :]
    vb = v.astype(BF16)
    o = jnp.dot(scores.astype(BF16), vb, preferred_element_type=F32)
    o = o + _dot_nt((q * jnp.exp(b)).astype(BF16), st.astype(BF16))
    kd = (k * jnp.exp(b_end - b)).astype(BF16)
    st_new = st * jnp.exp(b_end) + _dot_tn(vb, kd)
    return o, st_new


def _hgrn2_kernel(q_ref, ff_ref, fb_ref, i_ref, g_ref, lbl_ref, nw_ref, trif_ref, trib_ref, lvlf_ref, lvlb_ref,
                  o_ref, st_ref, oacc_ref, *, layer, n_ctx_chunks):
    n_chunks = q_ref.shape[1] // CHUNK

    def lower_bound(d):
        lg = lbl_ref[d]
        e = jnp.exp(lg - jnp.max(lg, axis=0, keepdims=True))
        sm = e / jnp.sum(e, axis=0, keepdims=True)
        lb = jnp.zeros((1, lg.shape[1]), F32)
        for l2 in range(1, layer + 1):
            lb = lb + sm[l2:l2 + 1, :]
        return lb

    def rows(c):
        return pl.ds(pl.multiple_of(c * CHUNK, CHUNK), CHUNK)

    lb_f = lower_bound(0)
    st_ref[...] = jnp.zeros_like(st_ref)

    def fwd_body(c, carry):
        r = rows(c)
        o, st = _gla_chunk(q_ref[0, r, :].astype(F32), ff_ref[0, r, :].astype(F32), i_ref[0, r, :].astype(F32),
                           lb_f, st_ref[...], trif_ref[...], lvlf_ref[...], False)
        st_ref[...] = st
        oacc_ref[r, :] = o
        return carry

    lax.fori_loop(0, n_chunks, fwd_body, 0)

    lb_b = lower_bound(1)
    st_ref[...] = jnp.zeros_like(st_ref)
    nw = nw_ref[...]

    def bwd_body(i, carry):
        c = jnp.where(i < n_ctx_chunks, n_ctx_chunks - 1 - i, n_chunks - 1 - (i - n_ctx_chunks))
        r = rows(c)
        o, st = _gla_chunk(q_ref[0, r, :].astype(F32), fb_ref[0, r, :].astype(F32), i_ref[0, r, :].astype(F32),
                           lb_b, st_ref[...], trib_ref[...], lvlb_ref[...], True)
        st_ref[...] = st
        o = o + oacc_ref[r, :]
        g = g_ref[0, r, :].astype(F32)
        y = o * lax.rsqrt(jnp.mean(o * o, axis=-1, keepdims=True) + RMS_EPS) * nw * (g * _sigmoid(g))
        o_ref[0, r, :] = y.astype(o_ref.dtype)
        return carry

    lax.fori_loop(0, n_chunks, bwd_body, 0)


def _hgrn2(u, lb_logits, norm_w, layer, lc):
    bsz, t_tot, _ = u.shape
    depth = lb_logits.shape[1]
    tri_f = np.tril(np.ones((CHUNK, CHUNK), np.float32))
    lvl_f = _level_table()
    ca = COL_A // LANES
    seq = lambda off: pl.BlockSpec((1, t_tot, LANES), lambda b, h: (b, 0, ca + off * A_HEADS + h))
    const = lambda: pl.BlockSpec((CHUNK, CHUNK), lambda b, h: (0, 0))
    kern = functools.partial(_hgrn2_kernel, layer=layer, n_ctx_chunks=lc // CHUNK)
    return pl.pallas_call(
        kern,
        out_shape=jax.ShapeDtypeStruct((bsz, t_tot, A_KW), BF16),
        grid=(bsz, A_HEADS),
        in_specs=[seq(0), seq(1), seq(2), seq(3), seq(4),
                  pl.BlockSpec((2, depth, LANES), lambda b, h: (0, 0, h)),
                  pl.BlockSpec((1, LANES), lambda b, h: (0, 0)),
                  const(), const(), const(), const()],
        out_specs=pl.BlockSpec((1, t_tot, LANES), lambda b, h: (b, 0, h)),
        scratch_shapes=[pltpu.VMEM((A_DK, A_DK), F32), pltpu.VMEM((t_tot, LANES), F32)],
        compiler_params=_cparams(("parallel", "parallel")),
        name="hgrn2",
    )(u, u, u, u, u, lb_logits, norm_w.reshape(1, LANES),
      jnp.asarray(tri_f, BF16), jnp.asarray(tri_f.T, BF16), jnp.asarray(lvl_f), jnp.asarray(lvl_f.T))


def _ssd_tables():
    e3 = np.zeros((2, B_GROUPS, LANES, B_HPG * LANES), np.float32)
    e2 = np.zeros((2, B_GROUPS, LANES, B_HPG * B_HEADDIM), np.float32)
    for d in range(2):
        for g in range(B_GROUPS):
            for j in range(B_HPG):
                lane = 16 * d + B_HPG * g + j
                for piece in range(3):
                    e3[d, g, 32 * piece + lane, j * LANES:(j + 1) * LANES] = 1.0
                for piece in range(2):
                    e2[d, g, 32 * piece + lane, j * B_HEADDIM:(j + 1) * B_HEADDIM] = 1.0
    return e3, e2


def _ssd_chunk(xs, bm, cm, dtraw, bias, a_neg, state, tri, e3, e2, lane0, backward, cumt_ref, dtt_ref):
    c = xs.shape[0]
    lane = lax.broadcasted_iota(jnp.int32, (c, LANES), 1)
    dt = _softplus(dtraw + bias)
    cum = _cumsum_rows(tri, dt * a_neg)
    hi, mid, lo = _split3(cum)
    cum3 = jnp.where(lane < 32, hi, jnp.where(lane < 64, mid, lo))
    colb = jnp.dot(cum3, e3, preferred_element_type=F32)
    cumt_ref[...] = cum.T
    dtt_ref[...] = dt.T

    def expand(vals):
        vhi = vals.astype(BF16)
        vlo = (vals - vhi.astype(F32)).astype(BF16)
        return jnp.dot(jnp.where(lane < 32, vhi, vlo), e2, preferred_element_type=F32)

    cum_end = cum[0:1, :] if backward else cum[c - 1:c, :]
    decay_in = expand(jnp.exp(cum))
    gain = expand(jnp.exp(cum_end - cum) * dt)

    bmb = bm.astype(BF16)
    cmb = cm.astype(BF16)
    cb = _dot_nt(cmb, bmb)
    ti = lax.broadcasted_iota(jnp.int32, (c, c), 0)
    si = lax.broadcasted_iota(jnp.int32, (c, c), 1)
    causal = (si >= ti) if backward else (si <= ti)
    xsb = xs.astype(BF16)
    half = lax.broadcasted_iota(jnp.int32, (c, LANES), 1) < B_HEADDIM

    ys = []
    for pair in range(B_HPG // 2):
        xp = xsb[:, pair * LANES:(pair + 1) * LANES]
        yy = []
        for sub in range(2):
            j = 2 * pair + sub
            ln = lane0 + j
            diff = colb[:, j * LANES:(j + 1) * LANES] - cumt_ref[pl.ds(ln, 1), :]
            lmat = jnp.where(causal, jnp.exp(jnp.minimum(diff, 0.0)), 0.0)
            mm = (cb * lmat * dtt_ref[pl.ds(ln, 1), :]).astype(BF16)
            yy.append(jnp.dot(mm, xp, preferred_element_type=F32))
        ys.append(jnp.where(half, yy[0], yy[1]))
    y_intra = jnp.concatenate(ys, axis=1)

    y_inter = jnp.dot(cmb, state.astype(BF16), preferred_element_type=F32) * decay_in
    upd = _dot_tn(bmb, (xs * gain).astype(BF16))
    dec_row = decay_in[0:1, :] if backward else decay_in[c - 1:c, :]
    return y_intra + y_inter, state * dec_row + upd


def _ssd_kernel(z_ref, xs_ref, bm_ref, cm_ref, dt_ref, bias_ref, aneg_ref, dsk_ref, nw_ref,
                trif_ref, trib_ref, e3_ref, e2_ref, o_ref, st_ref, yacc_ref, cumt_ref, dtt_ref, *, n_ctx_chunks):
    g = pl.program_id(1)
    n_chunks = z_ref.shape[1] // CHUNK
    bias = bias_ref[...]
    a_neg = aneg_ref[...]

    def rows(c):
        return pl.ds(pl.multiple_of(c * CHUNK, CHUNK), CHUNK)

    def run(c, tri_ref, d, backward):
        r = rows(c)
        return _ssd_chunk(xs_ref[0, r, :].astype(F32), bm_ref[0, r, :], cm_ref[0, r, :], dt_ref[0, r, :],
                          bias, a_neg, st_ref[...], tri_ref[...], e3_ref[d, 0], e2_ref[d, 0],
                          16 * d + B_HPG * g, backward, cumt_ref, dtt_ref)

    st_ref[...] = jnp.zeros_like(st_ref)

    def fwd_body(c, carry):
        y, st = run(c, trif_ref, 0, False)
        st_ref[...] = st
        yacc_ref[rows(c), :] = y
        return carry

    lax.fori_loop(0, n_chunks, fwd_body, 0)

    st_ref[...] = jnp.zeros_like(st_ref)
    dsk = dsk_ref[...]
    nw = nw_ref[...]

    def bwd_body(i, carry):
        c = jnp.where(i < n_ctx_chunks, n_ctx_chunks - 1 - i, n_chunks - 1 - (i - n_ctx_chunks))
        r = rows(c)
        y, st = run(c, trib_ref, 1, True)
        st_ref[...] = st
        z = z_ref[0, r, :].astype(F32)
        y = y + yacc_ref[r, :] + dsk * xs_ref[0, r, :].astype(F32)
        v = y * (z * _sigmoid(z))
        o_ref[0, r, :] = (v * lax.rsqrt(jnp.mean(v * v, axis=-1, keepdims=True) + RMS_EPS) * nw).astype(o_ref.dtype)
        return carry

    lax.fori_loop(0, n_chunks, bwd_body, 0)


def _ssd(u, dtraw, dt_bias, a_log, d_skip, norm_w, lc):
    bsz, t_tot, _ = u.shape
    gw = B_HPG * B_HEADDIM
    tri_f = np.tril(np.ones((CHUNK, CHUNK), np.float32))
    e3, e2 = _ssd_tables()
    rep = LANES // (2 * B_HEADS)
    bias_row = jnp.tile(dt_bias.reshape(1, 2 * B_HEADS), (1, rep))
    aneg_row = jnp.tile(a_log.reshape(1, 2 * B_HEADS), (1, rep))
    aneg_row = -jnp.exp(aneg_row.astype(F32))
    dsk_row = jnp.repeat(d_skip, B_HEADDIM).reshape(1, B_WIDTH)
    cz, cx = COL_Z // gw, COL_XBC // gw
    cbm = (COL_XBC + B_WIDTH) // LANES
    ccm = cbm + B_GROUPS
    const2 = lambda shape: pl.BlockSpec(shape, lambda b, g: (0,) * len(shape))
    kern = functools.partial(_ssd_kernel, n_ctx_chunks=lc // CHUNK)
    return pl.pallas_call(
        kern,
        out_shape=jax.ShapeDtypeStruct((bsz, t_tot, B_WIDTH), BF16),
        grid=(bsz, B_GROUPS),
        in_specs=[pl.BlockSpec((1, t_tot, gw), lambda b, g: (b, 0, cz + g)),
                  pl.BlockSpec((1, t_tot, gw), lambda b, g: (b, 0, cx + g)),
                  pl.BlockSpec((1, t_tot, LANES), lambda b, g: (b, 0, cbm + g)),
                  pl.BlockSpec((1, t_tot, LANES), lambda b, g: (b, 0, ccm + g)),
                  pl.BlockSpec((1, t_tot, LANES), lambda b, g: (b, 0, 0)),
                  const2((1, LANES)), const2((1, LANES)),
                  pl.BlockSpec((1, gw), lambda b, g: (0, g)),
                  pl.BlockSpec((1, gw), lambda b, g: (0, g)),
                  const2((CHUNK, CHUNK)), const2((CHUNK, CHUNK)),
                  pl.BlockSpec((2, 1, LANES, B_HPG * LANES), lambda b, g: (0, g, 0, 0)),
                  pl.BlockSpec((2, 1, LANES, gw), lambda b, g: (0, g, 0, 0))],
        out_specs=pl.BlockSpec((1, t_tot, gw), lambda b, g: (b, 0, g)),
        scratch_shapes=[pltpu.VMEM((B_STATE, gw), F32), pltpu.VMEM((t_tot, gw), F32),
                        pltpu.VMEM((LANES, CHUNK), F32), pltpu.VMEM((LANES, CHUNK), F32)],
        compiler_params=_cparams(("parallel", "parallel")),
        name="ssd",
    )(u, u, u, u, dtraw, bias_row, aneg_row, dsk_row, norm_w.reshape(1, B_WIDTH),
      jnp.asarray(tri_f, BF16), jnp.asarray(tri_f.T, BF16), jnp.asarray(e3, BF16), jnp.asarray(e2, BF16))


def _merge_kernel(x_ref, mod_ref, ya_ref, yb_ref, ga_ref, gb_ref, wpa_ref, wpb_ref, wo_ref, lng_ref, lnb_ref,
                  o_ref, *, lc, t_off, alpha):
    t = pl.program_id(1)
    tm = x_ref.shape[1]
    pa = jnp.dot(ya_ref[0], wpa_ref[...], preferred_element_type=F32)
    pb = jnp.dot(yb_ref[0], wpb_ref[...], preferred_element_type=F32)
    merged = _sigmoid(ga_ref[0].astype(F32)) * pa + _sigmoid(gb_ref[0].astype(F32)) * pb
    out = jnp.dot(merged.astype(BF16), wo_ref[...], preferred_element_type=F32)
    tok = (t + t_off) * tm + lax.broadcasted_iota(jnp.int32, (tm, 1), 0)
    gate = jnp.where(tok < lc, mod_ref[0, 1, 2:3, :], mod_ref[0, 0, 2:3, :])
    y = alpha * x_ref[0] + gate * out
    mu = jnp.mean(y, axis=-1, keepdims=True)
    yc = y - mu
    var = jnp.mean(yc * yc, axis=-1, keepdims=True)
    o_ref[0] = yc * lax.rsqrt(var + LN_EPS) * lng_ref[...] + lnb_ref[...]


def _merge(xcat, mod, ya, yb, u, wpa, wpb, wo, ln_g, ln_b, lc, alpha, latent_only):
    bsz, t_tot, d = xcat.shape
    tm = 256
    t_off = lc // tm if latent_only else 0
    n_t = t_tot // tm - t_off
    seq = lambda w, cb=0: pl.BlockSpec((1, tm, w), lambda b, t: (b, t + t_off, cb))
    full = lambda a: pl.BlockSpec(a.shape, lambda b, t: (0,) * a.ndim)
    kern = functools.partial(_merge_kernel, lc=lc, t_off=t_off, alpha=alpha)
    lng = ln_g.reshape(1, d)
    lnb = ln_b.reshape(1, d)
    return pl.pallas_call(
        kern,
        out_shape=jax.ShapeDtypeStruct((bsz, n_t * tm, d), F32),
        grid=(bsz, n_t),
        in_specs=[seq(d), pl.BlockSpec((1, 2, 3, d), lambda b, t: (b, 0, 0, 0)),
                  seq(A_KW), seq(B_WIDTH), seq(d, 0), seq(d, 1),
                  full(wpa), full(wpb), full(wo), full(lng), full(lnb)],
        out_specs=pl.BlockSpec((1, tm, d), lambda b, t: (b, t, 0)),
        compiler_params=_cparams(("parallel", "parallel")),
        name="merge",
    )(xcat, mod, ya, yb, u, u, wpa, wpb, wo, lng, lnb)


def kernel(x, c, ctx, c_ctx, w_mod, b_mod, w_in, a_lb_logits, a_norm_w, b_conv_w, b_conv_b, b_dt_bias, b_a_log,
           b_d, b_norm_w, w_proj_a, w_proj_b, w_out, ln_g, ln_b):
    bsz, seq, d = x.shape
    lc = ctx.shape[1]
    depth = w_mod.shape[0]
    assert 2 * d == COL_A - COL_GATE and d == 2 * A_KW and seq % GRID_W == 0
    assert lc % 256 == 0 and (lc + seq) % 256 == 0 and seq % CHUNK == 0
    alpha = (2 * depth) ** 0.25

    rows = -(-(bsz + 1) // 8) * 8
    cc = jnp.zeros((rows, d), F32).at[:bsz].set(c).at[bsz].set(c_ctx)
    mod_all = _modulation(cc, w_mod, b_mod)
    ml = mod_all[:, :bsz].reshape(depth, bsz, 1, 3, d)
    mc = jnp.broadcast_to(mod_all[:, bsz].reshape(depth, 1, 1, 3, d), (depth, bsz, 1, 3, d))
    mod_all = jnp.concatenate([ml, mc], axis=2)

    o_dt = 5 * A_KW + B_WIDTH + (B_WIDTH + 2 * B_GROUPS * B_STATE)
    o_gate = o_dt + 2 * B_HEADS
    w_u = jnp.concatenate([w_in[:, :, o_gate:], w_in[:, :, :o_dt]], axis=2).astype(BF16)
    w_dt = jnp.tile(w_in[:, :, o_dt:o_gate], (1, 1, LANES // (2 * B_HEADS))).astype(BF16)
    wpa = w_proj_a.astype(BF16)
    wpb = w_proj_b.astype(BF16)
    wo = w_out.astype(BF16)

    xcat = jnp.concatenate([ctx, x], axis=1)
    for l in range(depth):
        mod = mod_all[l]
        u = _inproj(xcat, mod, w_u[l], b_conv_w[l], b_conv_b[l].reshape(1, -1), lc)
        dtraw = _dtproj(xcat, mod, w_dt[l], lc)
        ya = _hgrn2(u, a_lb_logits, a_norm_w[l], l, lc)
        yb = _ssd(u, dtraw, b_dt_bias[l], b_a_log[l], b_d[l], b_norm_w[l], lc)
        xcat = _merge(xcat, mod, ya, yb, u, wpa[l], wpb[l], wo[l], ln_g[l], ln_b[l], lc, alpha,
                      latent_only=(l == depth - 1))
    return xcat
```

```python
import functools

import numpy as np
import jax
import jax.numpy as jnp
from jax import lax
from jax.experimental import pallas as pl
from jax.experimental.pallas import tpu as pltpu

F32 = jnp.float32
BF16 = jnp.bfloat16

A_HEADS = 4
A_DK = 128
A_KW = A_HEADS * A_DK
B_WIDTH = 1024
B_HEADDIM = 64
B_HEADS = B_WIDTH // B_HEADDIM
B_GROUPS = 4
B_HPG = B_HEADS // B_GROUPS
B_STATE = 128
B_CONV = 5
GRID_W = 64
LN_EPS = 1e-5
RMS_EPS = 1e-6
F_FLOOR = 1e-30

CHUNK = 128
LANES = 128
VMEM_LIMIT = 56 * 1024 * 1024

COL_GATE = 0
COL_A = 2048
COL_Z = COL_A + 5 * A_KW
COL_XBC = COL_Z + B_WIDTH
N_U = COL_XBC + B_WIDTH + 2 * B_GROUPS * B_STATE


def _sigmoid(x):
    return 1.0 / (1.0 + jnp.exp(-x))


def _softplus(x):
    return jnp.maximum(x, 0.0) + jnp.log(1.0 + jnp.exp(-jnp.abs(x)))


def _split3(x):
    hi = x.astype(BF16)
    r = x - hi.astype(F32)
    mid = r.astype(BF16)
    lo = (r - mid.astype(F32)).astype(BF16)
    return hi, mid, lo


def _cumsum_rows(tri, x):
    hi, mid, lo = _split3(x)
    d = lambda a: jnp.dot(tri, a, preferred_element_type=F32)
    return d(hi) + d(mid) + d(lo)


def _dot_nt(a, b):
    return lax.dot_general(a, b, (((1,), (1,)), ((), ())), preferred_element_type=F32)


def _dot_tn(a, b):
    return lax.dot_general(a, b, (((0,), (0,)), ((), ())), preferred_element_type=F32)


def _cparams(sem):
    return pltpu.CompilerParams(dimension_semantics=sem, vmem_limit_bytes=VMEM_LIMIT)


def _mod_kernel(c_ref, w_ref, b_ref, o_ref):
    a = c_ref[...]
    a = a * _sigmoid(a)
    o_ref[0] = jnp.dot(a, w_ref[0], preferred_element_type=F32) + b_ref[0]


def _modulation(cc, w_mod, b_mod):
    depth, d, d3 = w_mod.shape
    rows = cc.shape[0]
    tn = 1024
    return pl.pallas_call(
        _mod_kernel,
        out_shape=jax.ShapeDtypeStruct((depth, rows, d3), F32),
        grid=(depth, d3 // tn),
        in_specs=[pl.BlockSpec((rows, d), lambda l, j: (0, 0)),
                  pl.BlockSpec((1, d, tn), lambda l, j: (l, 0, j)),
                  pl.BlockSpec((1, 1, tn), lambda l, j: (l, 0, j))],
        out_specs=pl.BlockSpec((1, rows, tn), lambda l, j: (l, 0, j)),
        compiler_params=_cparams(("arbitrary", "arbitrary")),
        name="modulation",
    )(cc, w_mod, b_mod.reshape(depth, 1, d3))


def _modulate(x, mod_ref, tok0, lc):
    tm = x.shape[0]
    tok = tok0 + lax.broadcasted_iota(jnp.int32, (tm, 1), 0)
    is_ctx = tok < lc
    shift = jnp.where(is_ctx, mod_ref[0, 1, 0:1, :], mod_ref[0, 0, 0:1, :])
    scale = jnp.where(is_ctx, mod_ref[0, 1, 1:2, :], mod_ref[0, 0, 1:2, :])
    return x * (1.0 + scale) + shift


def _shift_vreg_rows(rot, ctx_first, n_ctx_v, down):
    nv = rot.shape[0]
    vpr = GRID_W // 8
    zero = jnp.zeros_like(rot[0:1])
    pieces = []
    for g0 in range(0, nv, vpr):
        g1 = g0 + vpr
        if down:
            edge = jnp.where(ctx_first, rot[g0 - 1:g0], 0.0) if 0 < g0 < n_ctx_v else zero
            pieces += [edge, rot[g0:g1 - 1]]
        else:
            edge = jnp.where(ctx_first, rot[g1:g1 + 1], 0.0) if g1 < n_ctx_v else zero
            pieces += [rot[g0 + 1:g1], edge]
    return jnp.concatenate(pieces, axis=0)


def _conv_rows(acc, cw_ref, cb_ref, ctx_first, n_ctx_v):
    n, tn = acc.shape
    a3 = acc.reshape(n // 8, 8, tn)
    sub = lax.broadcasted_iota(jnp.int32, a3.shape, 1)
    pad = B_CONV // 2
    out = a3 * cw_ref[pad:pad + 1, :] + cb_ref[...]
    for d in range(1, pad + 1):
        rot = pltpu.roll(a3, d, axis=1)
        prev = _shift_vreg_rows(rot, ctx_first, n_ctx_v, True)
        out = out + jnp.where(sub >= d, rot, prev) * cw_ref[pad - d:pad - d + 1, :]
        rot = pltpu.roll(a3, 8 - d, axis=1)
        nxt = _shift_vreg_rows(rot, ctx_first, n_ctx_v, False)
        out = out + jnp.where(sub < 8 - d, rot, nxt) * cw_ref[pad + d:pad + d + 1, :]
    return out.reshape(n, tn)


def _inproj_kernel(x_ref, mod_ref, w_ref, cw_ref, cb_ref, o_ref, h_ref, *, lc, conv_lo, conv_hi):
    t = pl.program_id(1)
    j = pl.program_id(2)
    tm = x_ref.shape[1]

    @pl.when(j == 0)
    def _():
        h_ref[...] = _modulate(x_ref[0], mod_ref, t * tm, lc).astype(BF16)

    acc = jnp.dot(h_ref[...], w_ref[...], preferred_element_type=F32)
    is_conv = jnp.logical_and(j >= conv_lo, j < conv_hi)

    @pl.when(is_conv)
    def _():
        out = _conv_rows(acc, cw_ref, cb_ref, t == 0, lc // 8)
        o_ref[0] = (out * _sigmoid(out)).astype(o_ref.dtype)

    @pl.when(jnp.logical_not(is_conv))
    def _():
        o_ref[0] = acc.astype(o_ref.dtype)


def _pick_tile(total, candidates):
    for c in candidates:
        if total % c == 0:
            return c
    raise ValueError(f"no tile for {total}")


def _inproj(xcat, mod, w_u, conv_w, conv_b, lc):
    bsz, t_tot, d = xcat.shape
    n_u = w_u.shape[1]
    tm = _pick_tile(t_tot, (768, 512, 256))
    tn = 512
    conv_lo, conv_hi = COL_XBC // tn, n_u // tn
    n_conv = conv_hi - conv_lo
    cmap = lambda b, t, j: (0, jnp.clip(j - conv_lo, 0, n_conv - 1))
    kern = functools.partial(_inproj_kernel, lc=lc, conv_lo=conv_lo, conv_hi=conv_hi)
    return pl.pallas_call(
        kern,
        out_shape=jax.ShapeDtypeStruct((bsz, t_tot, n_u), BF16),
        grid=(bsz, t_tot // tm, n_u // tn),
        in_specs=[pl.BlockSpec((1, tm, d), lambda b, t, j: (b, t, 0)),
                  pl.BlockSpec((1, 2, 3, d), lambda b, t, j: (b, 0, 0, 0)),
                  pl.BlockSpec((d, tn), lambda b, t, j: (0, j)),
                  pl.BlockSpec((B_CONV, tn), cmap),
                  pl.BlockSpec((1, tn), cmap)],
        out_specs=pl.BlockSpec((1, tm, tn), lambda b, t, j: (b, t, j)),
        scratch_shapes=[pltpu.VMEM((tm, d), BF16)],
        compiler_params=_cparams(("parallel", "parallel", "arbitrary")),
        name="inproj",
    )(xcat, mod, w_u, conv_w, conv_b)


def _dtproj_kernel(x_ref, mod_ref, w_ref, bias_ref, aneg_ref, tril_ref, triu_ref, dt_ref, cum_ref, *, lc):
    t = pl.program_id(1)
    tm = x_ref.shape[1]
    h = _modulate(x_ref[0], mod_ref, t * tm, lc).astype(BF16)
    dt = _softplus(jnp.dot(h, w_ref[...], preferred_element_type=F32) + bias_ref[...])
    dt_ref[0] = dt
    da = dt * aneg_ref[...]
    fwd_lane = (lax.broadcasted_iota(jnp.int32, (CHUNK, LANES), 1) & B_HEADS) == 0
    for r0 in range(0, tm, CHUNK):
        blk = da[r0:r0 + CHUNK]
        cum_ref[0, r0:r0 + CHUNK, :] = jnp.where(fwd_lane, _cumsum_rows(tril_ref[...], blk),
                                                 _cumsum_rows(triu_ref[...], blk))


def _dtproj(xcat, mod, w_dt, dt_bias, a_log, lc):
    bsz, t_tot, d = xcat.shape
    tm = _pick_tile(t_tot, (768, 512, 256))
    rep = LANES // (2 * B_HEADS)
    bias_row = jnp.tile(dt_bias.reshape(1, 2 * B_HEADS), (1, rep))
    aneg_row = -jnp.exp(jnp.tile(a_log.reshape(1, 2 * B_HEADS), (1, rep)).astype(F32))
    tril = np.tril(np.ones((CHUNK, CHUNK), np.float32))
    full = lambda shape: pl.BlockSpec(shape, lambda b, t: (0,) * len(shape))
    seq = pl.BlockSpec((1, tm, LANES), lambda b, t: (b, t, 0))
    return pl.pallas_call(
        functools.partial(_dtproj_kernel, lc=lc),
        out_shape=(jax.ShapeDtypeStruct((bsz, t_tot, LANES), F32), jax.ShapeDtypeStruct((bsz, t_tot, LANES), F32)),
        grid=(bsz, t_tot // tm),
        in_specs=[pl.BlockSpec((1, tm, d), lambda b, t: (b, t, 0)),
                  pl.BlockSpec((1, 2, 3, d), lambda b, t: (b, 0, 0, 0)),
                  full((d, LANES)), full((1, LANES)), full((1, LANES)),
                  full((CHUNK, CHUNK)), full((CHUNK, CHUNK))],
        out_specs=(seq, seq),
        compiler_params=_cparams(("parallel", "parallel")),
        name="dtproj",
    )(xcat, mod, w_dt, bias_row, aneg_row, jnp.asarray(tril, BF16), jnp.asarray(tril.T, BF16))


def _level_table():
    t = np.arange(CHUNK)[:, None]
    s = np.arange(CHUNK)[None, :]
    x = t ^ s
    lvl = np.where(x > 0, 2 ** np.floor(np.log2(np.maximum(x, 1))).astype(np.int64), 0)
    return np.where(s <= t, lvl, -1).astype(np.int32)


def _boundary_rows(b, m, backward):
    n = b.shape[0]
    off = m if backward else m - 1
    if m >= 8:
        parts = [jnp.broadcast_to(b[g0 + off:g0 + off + 1, :], (2 * m, b.shape[1])) for g0 in range(0, n, 2 * m)]
        return parts[0] if len(parts) == 1 else jnp.concatenate(parts, axis=0)
    b3 = b.reshape(n // 8, 8, b.shape[1])
    sub = lax.broadcasted_iota(jnp.int32, b3.shape, 1)
    out = None
    for g0 in range(0, 8, 2 * m):
        src = jnp.broadcast_to(b3[:, g0 + off:g0 + off + 1, :], b3.shape)
        out = src if out is None else jnp.where(sub >= g0, src, out)
    return out.reshape(b.shape)


A_HEADS_PER_STEP = 4


def _hgrn2_kernel(q_ref, ff_ref, fb_ref, i_ref, g_ref, lbl_ref, nw_ref, trif_ref, trib_ref, lvlf_ref, lvlb_ref,
                  o_ref, st_ref, of_ref, ob_ref, *, layer, n_ctx_chunks):
    n_chunks = q_ref.shape[1] // CHUNK
    hps = q_ref.shape[2] // LANES

    def lower_bound(d):
        lg = lbl_ref[d]
        e = jnp.exp(lg - jnp.max(lg, axis=0, keepdims=True))
        sm = e / jnp.sum(e, axis=0, keepdims=True)
        lb = jnp.zeros((1, lg.shape[1]), F32)
        for l2 in range(1, layer + 1):
            lb = lb + sm[l2:l2 + 1, :]
        return lb

    def rows(c):
        return pl.ds(pl.multiple_of(c * CHUNK, CHUNK), CHUNK)

    lbs = (lower_bound(0), lower_bound(1))
    st_ref[...] = jnp.zeros_like(st_ref)

    z_refs, tri_refs, lvl_refs, acc_refs = (ff_ref, fb_ref), (trif_ref, trib_ref), (lvlf_ref, lvlb_ref), (of_ref, ob_ref)
    lanes = [slice(hh * LANES, (hh + 1) * LANES) for hh in range(hps)]

    def body(i, carry):
        c_b = jnp.where(i < n_ctx_chunks, n_ctx_chunks - 1 - i, n_chunks - 1 - (i - n_ctx_chunks))
        rws = (rows(i), rows(c_b))
        row = lax.broadcasted_iota(jnp.int32, (CHUNK, hps * LANES), 0)
        q, k, f, b, vb, lvl = [], [], [], [], [], []
        for d in range(2):
            fr = lbs[d] + (1.0 - lbs[d]) * _sigmoid(z_refs[d][0, rws[d], :].astype(F32))
            q.append(q_ref[0, rws[d], :].astype(F32))
            k.append(1.0 - fr)
            f.append(jnp.maximum(fr, F_FLOOR))
            b.append(_cumsum_rows(tri_refs[d][...], jnp.log(f[d])))
            vb.append(i_ref[0, rws[d], :])
            lvl.append(lvl_refs[d][...])

        chains = [(hh, d) for hh in range(hps) for d in range(2)]
        scores = {}
        for hh, d in chains:
            ls = lanes[hh]
            scores[hh, d] = jnp.where(lvl[d] == 0, _dot_nt(q[d][:, ls].astype(BF16), k[d][:, ls].astype(BF16)), 0.0)
        m = 1
        while m < CHUNK:
            w = []
            for d in range(2):
                bit = (row & m) != 0
                qside = jnp.logical_not(bit) if d == 1 else bit
                if m == 1:
                    e = jnp.where(qside, f[d], 1.0)
                else:
                    bm = _boundary_rows(b[d], m, d == 1)
                    e = jnp.exp(jnp.where(qside, b[d] - bm, bm - b[d]))
                w.append((jnp.where(qside, q[d], k[d]) * e).astype(BF16))
            for hh, d in chains:
                wh = w[d][:, lanes[hh]]
                scores[hh, d] = jnp.where(lvl[d] == m, _dot_nt(wh, wh), scores[hh, d])
            m *= 2

        qe, kdec, eb = [], [], []
        for d in range(2):
            b_end = b[d][0:1, :] if d == 1 else b[d][CHUNK - 1:CHUNK, :]
            qe.append((q[d] * jnp.exp(b[d])).astype(BF16))
            kdec.append((k[d] * jnp.exp(b_end - b[d])).astype(BF16))
            eb.append(jnp.exp(b_end))
        for hh, d in chains:
            ls = lanes[hh]
            st = st_ref[2 * hh + d]
            o = jnp.dot(scores[hh, d].astype(BF16), vb[d][:, ls], preferred_element_type=F32)
            o = o + _dot_nt(qe[d][:, ls], st.astype(BF16))
            st_ref[2 * hh + d] = st * eb[d][:, ls] + _dot_tn(vb[d][:, ls], kdec[d][:, ls])
            acc_refs[d][rws[d], ls] = o
        return carry

    lax.fori_loop(0, n_chunks, body, 0)

    nw = nw_ref[...]

    def finish(c, carry):
        r = rows(c)
        for hh in range(hps):
            ls = slice(hh * LANES, (hh + 1) * LANES)
            o = of_ref[r, ls] + ob_ref[r, ls]
            g = g_ref[0, r, ls].astype(F32)
            y = o * lax.rsqrt(jnp.mean(o * o, axis=-1, keepdims=True) + RMS_EPS) * nw * (g * _sigmoid(g))
            o_ref[0, r, ls] = y.astype(o_ref.dtype)
        return carry

    lax.fori_loop(0, n_chunks, finish, 0)


def _hgrn2(u, lb_logits, norm_w, layer, lc):
    bsz, t_tot, _ = u.shape
    depth = lb_logits.shape[1]
    hps = A_HEADS_PER_STEP
    wblk = hps * LANES
    tri_f = np.tril(np.ones((CHUNK, CHUNK), np.float32))
    lvl_f = _level_table()
    ca = COL_A // wblk
    seq = lambda off: pl.BlockSpec((1, t_tot, wblk), lambda b, h: (b, 0, ca + off * (A_HEADS // hps) + h))
    const = lambda: pl.BlockSpec((CHUNK, CHUNK), lambda b, h: (0, 0))
    kern = functools.partial(_hgrn2_kernel, layer=layer, n_ctx_chunks=lc // CHUNK)
    return pl.pallas_call(
        kern,
        out_shape=jax.ShapeDtypeStruct((bsz, t_tot, A_KW), BF16),
        grid=(bsz, A_HEADS // hps),
        in_specs=[seq(0), seq(1), seq(2), seq(3), seq(4),
                  pl.BlockSpec((2, depth, wblk), lambda b, h: (0, 0, h)),
                  pl.BlockSpec((1, LANES), lambda b, h: (0, 0)),
                  const(), const(), const(), const()],
        out_specs=pl.BlockSpec((1, t_tot, wblk), lambda b, h: (b, 0, h)),
        scratch_shapes=[pltpu.VMEM((2 * hps, A_DK, A_DK), F32),
                        pltpu.VMEM((t_tot, wblk), F32), pltpu.VMEM((t_tot, wblk), F32)],
        compiler_params=_cparams(("parallel", "parallel")),
        name="hgrn2",
    )(u, u, u, u, u, lb_logits, norm_w.reshape(1, LANES),
      jnp.asarray(tri_f, BF16), jnp.asarray(tri_f.T, BF16), jnp.asarray(lvl_f), jnp.asarray(lvl_f.T))


def _ssd_tables(gps):
    hps = gps * B_HPG
    e3 = np.zeros((2, B_HEADS // hps, LANES, hps * LANES), np.float32)
    e2 = np.zeros((2, B_HEADS // hps, LANES, hps * B_HEADDIM), np.float32)
    for d in range(2):
        for s in range(B_HEADS // hps):
            for jj in range(hps):
                lane = 16 * d + hps * s + jj
                for piece in range(3):
                    e3[d, s, 32 * piece + lane, jj * LANES:(jj + 1) * LANES] = 1.0
                for piece in range(2):
                    e2[d, s, 32 * piece + lane, jj * B_HEADDIM:(jj + 1) * B_HEADDIM] = 1.0
    return e3, e2


B_GROUPS_PER_STEP = 2


def _ssd_kernel(z_ref, xs_ref, bm_ref, cm_ref, dt_ref, cum_ref, dsk_ref, nw_ref, e3_ref, e2_ref,
                o_ref, st_ref, yf_ref, yb_ref, cumt_ref, dtt_ref, *, n_ctx_chunks):
    gw = B_HPG * B_HEADDIM
    gps = z_ref.shape[2] // gw
    g0 = pl.program_id(1) * gps
    n_chunks = z_ref.shape[1] // CHUNK
    c = CHUNK

    def rows(k):
        return pl.ds(pl.multiple_of(k * CHUNK, CHUNK), CHUNK)

    st_ref[...] = jnp.zeros_like(st_ref)
    accs = (yf_ref, yb_ref)

    def body(i, carry):
        c_b = jnp.where(i < n_ctx_chunks, n_ctx_chunks - 1 - i, n_chunks - 1 - (i - n_ctx_chunks))
        rws = (rows(i), rows(c_b))
        lane = lax.broadcasted_iota(jnp.int32, (c, LANES), 1)
        ti = lax.broadcasted_iota(jnp.int32, (c, c), 0)
        si = lax.broadcasted_iota(jnp.int32, (c, c), 1)
        causal = (si <= ti, si >= ti)
        half = lane < B_HEADDIM

        colb, decay_in, gain = [], [], []
        for d in range(2):
            dt = dt_ref[0, rws[d], :]
            cum = cum_ref[0, rws[d], :]
            hi, mid, lo = _split3(cum)
            cum3 = jnp.where(lane < 32, hi, jnp.where(lane < 64, mid, lo))
            colb.append(jnp.dot(cum3, e3_ref[d, 0], preferred_element_type=F32))
            cumt_ref[d] = cum.T
            dtt_ref[d] = dt.T

            def expand(vals):
                vhi = vals.astype(BF16)
                vlo = (vals - vhi.astype(F32)).astype(BF16)
                return jnp.dot(jnp.where(lane < 32, vhi, vlo), e2_ref[d, 0], preferred_element_type=F32)

            cum_end = cum[0:1, :] if d == 1 else cum[c - 1:c, :]
            decay_in.append(expand(jnp.exp(cum)))
            gain.append(expand(jnp.exp(jnp.minimum(cum_end - cum, 0.0)) * dt))

        chains = [(gg, d) for gg in range(gps) for d in range(2)]
        bmb, cmb, xsf, cbs = {}, {}, {}, {}
        for gg, d in chains:
            bmb[gg, d] = bm_ref[0, rws[d], gg * LANES:(gg + 1) * LANES]
            cmb[gg, d] = cm_ref[0, rws[d], gg * LANES:(gg + 1) * LANES]
            xsf[gg, d] = xs_ref[0, rws[d], gg * gw:(gg + 1) * gw]
            cbs[gg, d] = _dot_nt(cmb[gg, d], bmb[gg, d])
        y_intra = {}
        for gg, d in chains:
            ys = []
            for pair in range(B_HPG // 2):
                xp = xsf[gg, d][:, pair * LANES:(pair + 1) * LANES]
                yy = []
                for sub in range(2):
                    j = 2 * pair + sub
                    ln = 16 * d + B_HPG * (g0 + gg) + j
                    col = (gg * B_HPG + j) * LANES
                    diff = colb[d][:, col:col + LANES] - cumt_ref[d, pl.ds(ln, 1), :]
                    lmat = jnp.where(causal[d], jnp.exp(jnp.minimum(diff, 0.0)), 0.0)
                    mm = (cbs[gg, d] * lmat * dtt_ref[d, pl.ds(ln, 1), :]).astype(BF16)
                    yy.append(jnp.dot(mm, xp, preferred_element_type=F32))
                ys.append(jnp.where(half, yy[0], yy[1]))
            y_intra[gg, d] = jnp.concatenate(ys, axis=1)
        for gg, d in chains:
            k = 2 * gg + d
            cs = slice(gg * gw, (gg + 1) * gw)
            state = st_ref[k]
            dec = decay_in[d][:, cs]
            y_inter = jnp.dot(cmb[gg, d], state.astype(BF16), preferred_element_type=F32) * dec
            upd = _dot_tn(bmb[gg, d], (xsf[gg, d].astype(F32) * gain[d][:, cs]).astype(BF16))
            dec_row = dec[0:1, :] if d == 1 else dec[c - 1:c, :]
            st_ref[k] = state * dec_row + upd
            accs[d][rws[d], cs] = y_intra[gg, d] + y_inter
        return carry

    lax.fori_loop(0, n_chunks, body, 0)

    dsk = dsk_ref[...]
    nw = nw_ref[...]

    def finish(c, carry):
        r = rows(c)
        for gg in range(gps):
            cs = slice(gg * gw, (gg + 1) * gw)
            z = z_ref[0, r, cs].astype(F32)
            y = yf_ref[r, cs] + yb_ref[r, cs] + dsk[:, cs] * xs_ref[0, r, cs].astype(F32)
            v = y * (z * _sigmoid(z))
            v = v * lax.rsqrt(jnp.mean(v * v, axis=-1, keepdims=True) + RMS_EPS) * nw[:, cs]
            o_ref[0, r, cs] = v.astype(o_ref.dtype)
        return carry

    lax.fori_loop(0, n_chunks, finish, 0)


def _ssd(u, dt, cum, d_skip, norm_w, lc):
    bsz, t_tot, _ = u.shape
    gw = B_HPG * B_HEADDIM
    gps = B_GROUPS_PER_STEP
    e3, e2 = _ssd_tables(gps)
    dsk_row = jnp.repeat(d_skip, B_HEADDIM).reshape(1, B_WIDTH)
    wblk, nblk = gps * gw, gps * LANES
    cz, cx = COL_Z // wblk, COL_XBC // wblk
    cbm = (COL_XBC + B_WIDTH) // nblk
    ccm = cbm + B_GROUPS // gps
    const2 = lambda shape: pl.BlockSpec(shape, lambda b, g: (0,) * len(shape))
    kern = functools.partial(_ssd_kernel, n_ctx_chunks=lc // CHUNK)
    return pl.pallas_call(
        kern,
        out_shape=jax.ShapeDtypeStruct((bsz, t_tot, B_WIDTH), BF16),
        grid=(bsz, B_GROUPS // gps),
        in_specs=[pl.BlockSpec((1, t_tot, wblk), lambda b, g: (b, 0, cz + g)),
                  pl.BlockSpec((1, t_tot, wblk), lambda b, g: (b, 0, cx + g)),
                  pl.BlockSpec((1, t_tot, nblk), lambda b, g: (b, 0, cbm + g)),
                  pl.BlockSpec((1, t_tot, nblk), lambda b, g: (b, 0, ccm + g)),
                  pl.BlockSpec((1, t_tot, LANES), lambda b, g: (b, 0, 0)),
                  pl.BlockSpec((1, t_tot, LANES), lambda b, g: (b, 0, 0)),
                  pl.BlockSpec((1, wblk), lambda b, g: (0, g)),
                  pl.BlockSpec((1, wblk), lambda b, g: (0, g)),
                  pl.BlockSpec((2, 1, LANES, gps * B_HPG * LANES), lambda b, g: (0, g, 0, 0)),
                  pl.BlockSpec((2, 1, LANES, wblk), lambda b, g: (0, g, 0, 0))],
        out_specs=pl.BlockSpec((1, t_tot, wblk), lambda b, g: (b, 0, g)),
        scratch_shapes=[pltpu.VMEM((2 * gps, B_STATE, gw), F32),
                        pltpu.VMEM((t_tot, wblk), F32), pltpu.VMEM((t_tot, wblk), F32),
                        pltpu.VMEM((2, LANES, CHUNK), F32), pltpu.VMEM((2, LANES, CHUNK), F32)],
        compiler_params=_cparams(("parallel", "parallel")),
        name="ssd",
    )(u, u, u, u, dt, cum, dsk_row, norm_w.reshape(1, B_WIDTH), jnp.asarray(e3, BF16), jnp.asarray(e2, BF16))


def _merge_kernel(x_ref, mod_ref, ya_ref, yb_ref, ga_ref, gb_ref, wpa_ref, wpb_ref, wo_ref, lng_ref, lnb_ref,
                  o_ref, *, lc, t_off, alpha):
    t = pl.program_id(1)
    tm = x_ref.shape[1]
    pa = jnp.dot(ya_ref[0], wpa_ref[...], preferred_element_type=F32)
    pb = jnp.dot(yb_ref[0], wpb_ref[...], preferred_element_type=F32)
    merged = _sigmoid(ga_ref[0].astype(F32)) * pa + _sigmoid(gb_ref[0].astype(F32)) * pb
    out = jnp.dot(merged.astype(BF16), wo_ref[...], preferred_element_type=F32)
    tok = (t + t_off) * tm + lax.broadcasted_iota(jnp.int32, (tm, 1), 0)
    gate = jnp.where(tok < lc, mod_ref[0, 1, 2:3, :], mod_ref[0, 0, 2:3, :])
    y = alpha * x_ref[0] + gate * out
    mu = jnp.mean(y, axis=-1, keepdims=True)
    yc = y - mu
    var = jnp.mean(yc * yc, axis=-1, keepdims=True)
    o_ref[0] = yc * lax.rsqrt(var + LN_EPS) * lng_ref[...] + lnb_ref[...]


def _merge(xcat, mod, ya, yb, u, wpa, wpb, wo, ln_g, ln_b, lc, alpha, latent_only):
    bsz, t_tot, d = xcat.shape
    tm = 256
    t_off = lc // tm if latent_only else 0
    n_t = t_tot // tm - t_off
    seq = lambda w, cb=0: pl.BlockSpec((1, tm, w), lambda b, t: (b, t + t_off, cb))
    full = lambda a: pl.BlockSpec(a.shape, lambda b, t: (0,) * a.ndim)
    kern = functools.partial(_merge_kernel, lc=lc, t_off=t_off, alpha=alpha)
    lng = ln_g.reshape(1, d)
    lnb = ln_b.reshape(1, d)
    return pl.pallas_call(
        kern,
        out_shape=jax.ShapeDtypeStruct((bsz, n_t * tm, d), F32),
        grid=(bsz, n_t),
        in_specs=[seq(d), pl.BlockSpec((1, 2, 3, d), lambda b, t: (b, 0, 0, 0)),
                  seq(A_KW), seq(B_WIDTH), seq(d, 0), seq(d, 1),
                  full(wpa), full(wpb), full(wo), full(lng), full(lnb)],
        out_specs=pl.BlockSpec((1, tm, d), lambda b, t: (b, t, 0)),
        compiler_params=_cparams(("parallel", "parallel")),
        name="merge",
    )(xcat, mod, ya, yb, u, u, wpa, wpb, wo, lng, lnb)


def kernel(x, c, ctx, c_ctx, w_mod, b_mod, w_in, a_lb_logits, a_norm_w, b_conv_w, b_conv_b, b_dt_bias, b_a_log,
           b_d, b_norm_w, w_proj_a, w_proj_b, w_out, ln_g, ln_b):
    bsz, seq, d = x.shape
    lc = ctx.shape[1]
    depth = w_mod.shape[0]
    assert 2 * d == COL_A - COL_GATE and d == 2 * A_KW and seq % GRID_W == 0
    assert lc % 256 == 0 and (lc + seq) % 256 == 0 and seq % CHUNK == 0
    alpha = (2 * depth) ** 0.25

    rows = -(-(bsz + 1) // 8) * 8
    cc = jnp.zeros((rows, d), F32).at[:bsz].set(c).at[bsz].set(c_ctx)
    mod_all = _modulation(cc, w_mod, b_mod)
    ml = mod_all[:, :bsz].reshape(depth, bsz, 1, 3, d)
    mc = jnp.broadcast_to(mod_all[:, bsz].reshape(depth, 1, 1, 3, d), (depth, bsz, 1, 3, d))
    mod_all = jnp.concatenate([ml, mc], axis=2)

    o_dt = 5 * A_KW + B_WIDTH + (B_WIDTH + 2 * B_GROUPS * B_STATE)
    o_gate = o_dt + 2 * B_HEADS
    w_u = jnp.concatenate([w_in[:, :, o_gate:], w_in[:, :, :o_dt]], axis=2).astype(BF16)
    w_dt = jnp.tile(w_in[:, :, o_dt:o_gate], (1, 1, LANES // (2 * B_HEADS))).astype(BF16)
    wpa = w_proj_a.astype(BF16)
    wpb = w_proj_b.astype(BF16)
    wo = w_out.astype(BF16)

    xcat = jnp.concatenate([ctx, x], axis=1)
    for l in range(depth):
        mod = mod_all[l]
        u = _inproj(xcat, mod, w_u[l], b_conv_w[l], b_conv_b[l].reshape(1, -1), lc)
        dt, cum = _dtproj(xcat, mod, w_dt[l], b_dt_bias[l], b_a_log[l], lc)
        ya = _hgrn2(u, a_lb_logits, a_norm_w[l], l, lc)
        yb = _ssd(u, dt, cum, b_d[l], b_norm_w[l], lc)
        xcat = _merge(xcat, mod, ya, yb, u, wpa[l], wpb[l], wo[l], ln_g[l], ln_b[l], lc, alpha,
                      latent_only=(l == depth - 1))
    return xcat
```

```python
import functools

import numpy as np
import jax
import jax.numpy as jnp
from jax import lax
from jax.experimental import pallas as pl
from jax.experimental.pallas import tpu as pltpu

F32 = jnp.float32
BF16 = jnp.bfloat16

A_HEADS = 4
A_DK = 128
A_KW = A_HEADS * A_DK
B_WIDTH = 1024
B_HEADDIM = 64
B_HEADS = B_WIDTH // B_HEADDIM
B_GROUPS = 4
B_HPG = B_HEADS // B_GROUPS
B_STATE = 128
B_CONV = 5
GRID_W = 64
LN_EPS = 1e-5
RMS_EPS = 1e-6
F_FLOOR = 1e-30

CHUNK = 128
LANES = 128
VMEM_LIMIT = 56 * 1024 * 1024

COL_GATE = 0
COL_A = 2048
COL_Z = COL_A + 5 * A_KW
COL_XBC = COL_Z + B_WIDTH
N_U = COL_XBC + B_WIDTH + 2 * B_GROUPS * B_STATE


def _sigmoid(x):
    return 1.0 / (1.0 + jnp.exp(-x))


def _softplus(x):
    return jnp.maximum(x, 0.0) + jnp.log(1.0 + jnp.exp(-jnp.abs(x)))


def _split3(x):
    hi = x.astype(BF16)
    r = x - hi.astype(F32)
    mid = r.astype(BF16)
    lo = (r - mid.astype(F32)).astype(BF16)
    return hi, mid, lo


def _cumsum_rows(tri, x):
    hi, mid, lo = _split3(x)
    d = lambda a: jnp.dot(tri, a, preferred_element_type=F32)
    return d(hi) + d(mid) + d(lo)


def _dot_nt(a, b):
    return lax.dot_general(a, b, (((1,), (1,)), ((), ())), preferred_element_type=F32)


def _dot_tn(a, b):
    return lax.dot_general(a, b, (((0,), (0,)), ((), ())), preferred_element_type=F32)


def _cparams(sem):
    return pltpu.CompilerParams(dimension_semantics=sem, vmem_limit_bytes=VMEM_LIMIT)


def _mod_kernel(c_ref, w_ref, b_ref, o_ref):
    a = c_ref[...]
    a = a * _sigmoid(a)
    o_ref[0] = jnp.dot(a, w_ref[0], preferred_element_type=F32) + b_ref[0]


def _modulation(cc, w_mod, b_mod):
    depth, d, d3 = w_mod.shape
    rows = cc.shape[0]
    tn = 1024
    return pl.pallas_call(
        _mod_kernel,
        out_shape=jax.ShapeDtypeStruct((depth, rows, d3), F32),
        grid=(depth, d3 // tn),
        in_specs=[pl.BlockSpec((rows, d), lambda l, j: (0, 0)),
                  pl.BlockSpec((1, d, tn), lambda l, j: (l, 0, j)),
                  pl.BlockSpec((1, 1, tn), lambda l, j: (l, 0, j))],
        out_specs=pl.BlockSpec((1, rows, tn), lambda l, j: (l, 0, j)),
        compiler_params=_cparams(("arbitrary", "arbitrary")),
        name="modulation",
    )(cc, w_mod, b_mod.reshape(depth, 1, d3))


def _modulate(x, mod_ref, tok0, lc):
    tm = x.shape[0]
    tok = tok0 + lax.broadcasted_iota(jnp.int32, (tm, 1), 0)
    is_ctx = tok < lc
    shift = jnp.where(is_ctx, mod_ref[0, 1, 0:1, :], mod_ref[0, 0, 0:1, :])
    scale = jnp.where(is_ctx, mod_ref[0, 1, 1:2, :], mod_ref[0, 0, 1:2, :])
    return x * (1.0 + scale) + shift


def _shift_vreg_rows(rot, ctx_first, n_ctx_v, down):
    nv = rot.shape[0]
    vpr = GRID_W // 8
    zero = jnp.zeros_like(rot[0:1])
    pieces = []
    for g0 in range(0, nv, vpr):
        g1 = g0 + vpr
        if down:
            edge = jnp.where(ctx_first, rot[g0 - 1:g0], 0.0) if 0 < g0 < n_ctx_v else zero
            pieces += [edge, rot[g0:g1 - 1]]
        else:
            edge = jnp.where(ctx_first, rot[g1:g1 + 1], 0.0) if g1 < n_ctx_v else zero
            pieces += [rot[g0 + 1:g1], edge]
    return jnp.concatenate(pieces, axis=0)


def _conv_rows(acc, cw, cb, ctx_first, n_ctx_v):
    n, tn = acc.shape
    a3 = acc.reshape(n // 8, 8, tn)
    sub = lax.broadcasted_iota(jnp.int32, a3.shape, 1)
    pad = B_CONV // 2
    out = a3 * cw[pad:pad + 1, :] + cb
    for d in range(1, pad + 1):
        rot = pltpu.roll(a3, d, axis=1)
        prev = _shift_vreg_rows(rot, ctx_first, n_ctx_v, True)
        out = out + jnp.where(sub >= d, rot, prev) * cw[pad - d:pad - d + 1, :]
        rot = pltpu.roll(a3, 8 - d, axis=1)
        nxt = _shift_vreg_rows(rot, ctx_first, n_ctx_v, False)
        out = out + jnp.where(sub < 8 - d, rot, nxt) * cw[pad + d:pad + d + 1, :]
    return out.reshape(n, tn)


INPROJ_TN = 512


def _inproj_kernel(x_ref, mod_ref, w_ref, wdt_ref, cw_ref, cb_ref, bias_ref, aneg_ref, tril_ref, triu_ref,
                   u_ref, dt_ref, cum_ref, h_ref, *, lc):
    t = pl.program_id(1)
    tm = x_ref.shape[1]
    n_u = w_ref.shape[1]
    h_ref[...] = _modulate(x_ref[0], mod_ref, t * tm, lc).astype(BF16)
    plain = list(range(0, COL_XBC, INPROJ_TN))
    conv = list(range(COL_XBC, n_u, INPROJ_TN))
    order = []
    while plain or conv:
        if conv:
            order.append(conv.pop(0))
        order += plain[:2]
        plain = plain[2:]
    for j0 in order:
        j1 = j0 + INPROJ_TN
        acc = jnp.dot(h_ref[...], w_ref[:, j0:j1], preferred_element_type=F32)
        if j0 >= COL_XBC:
            acc = _conv_rows(acc, cw_ref[:, j0 - COL_XBC:j1 - COL_XBC], cb_ref[:, j0 - COL_XBC:j1 - COL_XBC],
                             t == 0, lc // 8)
            acc = acc * _sigmoid(acc)
        u_ref[0, :, j0:j1] = acc.astype(u_ref.dtype)

    dt = _softplus(jnp.dot(h_ref[...], wdt_ref[...], preferred_element_type=F32) + bias_ref[...])
    dt_ref[0] = dt
    da = dt * aneg_ref[...]
    fwd_lane = (lax.broadcasted_iota(jnp.int32, (CHUNK, LANES), 1) & B_HEADS) == 0
    for r0 in range(0, tm, CHUNK):
        blk = da[r0:r0 + CHUNK]
        cum_ref[0, r0:r0 + CHUNK, :] = jnp.where(fwd_lane, _cumsum_rows(tril_ref[...], blk),
                                                 _cumsum_rows(triu_ref[...], blk))


def _pick_tile(total, candidates):
    for c in candidates:
        if total % c == 0:
            return c
    raise ValueError(f"no tile for {total}")


def _inproj(xcat, mod, w_u, w_dt, conv_w, conv_b, dt_bias, a_log, lc):
    bsz, t_tot, d = xcat.shape
    n_u = w_u.shape[1]
    tm = _pick_tile(t_tot, (768, 512, 256))
    assert lc <= tm and COL_XBC % INPROJ_TN == 0 and n_u % INPROJ_TN == 0
    rep = LANES // (2 * B_HEADS)
    bias_row = jnp.tile(dt_bias.reshape(1, 2 * B_HEADS), (1, rep))
    aneg_row = -jnp.exp(jnp.tile(a_log.reshape(1, 2 * B_HEADS), (1, rep)).astype(F32))
    tril = np.tril(np.ones((CHUNK, CHUNK), np.float32))
    full = lambda shape: pl.BlockSpec(shape, lambda b, t: (0,) * len(shape))
    resident = lambda shape: pl.BlockSpec(shape, lambda b, t: (0,) * len(shape), pipeline_mode=pl.Buffered(1))
    seq = lambda w: pl.BlockSpec((1, tm, w), lambda b, t: (b, t, 0))
    return pl.pallas_call(
        functools.partial(_inproj_kernel, lc=lc),
        out_shape=(jax.ShapeDtypeStruct((bsz, t_tot, n_u), BF16),
                   jax.ShapeDtypeStruct((bsz, t_tot, LANES), F32), jax.ShapeDtypeStruct((bsz, t_tot, LANES), F32)),
        grid=(bsz, t_tot // tm),
        in_specs=[seq(d), pl.BlockSpec((1, 2, 3, d), lambda b, t: (b, 0, 0, 0)),
                  resident((d, n_u)), resident((d, LANES)),
                  full((B_CONV, n_u - COL_XBC)), full((1, n_u - COL_XBC)),
                  full((1, LANES)), full((1, LANES)), full((CHUNK, CHUNK)), full((CHUNK, CHUNK))],
        out_specs=(seq(n_u), seq(LANES), seq(LANES)),
        scratch_shapes=[pltpu.VMEM((tm, d), BF16)],
        compiler_params=_cparams(("parallel", "parallel")),
        name="inproj",
    )(xcat, mod, w_u, w_dt, conv_w, conv_b, bias_row, aneg_row, jnp.asarray(tril, BF16), jnp.asarray(tril.T, BF16))


def _level_table():
    t = np.arange(CHUNK)[:, None]
    s = np.arange(CHUNK)[None, :]
    x = t ^ s
    lvl = np.where(x > 0, 2 ** np.floor(np.log2(np.maximum(x, 1))).astype(np.int64), 0)
    return np.where(s <= t, lvl, -1).astype(np.int32)


def _boundary_rows(b, m, backward):
    n = b.shape[0]
    off = m if backward else m - 1
    if m >= 8:
        parts = [jnp.broadcast_to(b[g0 + off:g0 + off + 1, :], (2 * m, b.shape[1])) for g0 in range(0, n, 2 * m)]
        return parts[0] if len(parts) == 1 else jnp.concatenate(parts, axis=0)
    b3 = b.reshape(n // 8, 8, b.shape[1])
    sub = lax.broadcasted_iota(jnp.int32, b3.shape, 1)
    out = None
    for g0 in range(0, 8, 2 * m):
        src = jnp.broadcast_to(b3[:, g0 + off:g0 + off + 1, :], b3.shape)
        out = src if out is None else jnp.where(sub >= g0, src, out)
    return out.reshape(b.shape)


A_HEADS_PER_STEP = 4


def _hgrn2_kernel(q_ref, ff_ref, fb_ref, i_ref, g_ref, lbl_ref, nw_ref, trif_ref, trib_ref, lvlf_ref, lvlb_ref,
                  o_ref, st_ref, of_ref, ob_ref, *, layer, n_ctx_chunks):
    n_chunks = q_ref.shape[1] // CHUNK
    hps = q_ref.shape[2] // LANES

    def lower_bound(d):
        lg = lbl_ref[d]
        e = jnp.exp(lg - jnp.max(lg, axis=0, keepdims=True))
        sm = e / jnp.sum(e, axis=0, keepdims=True)
        lb = jnp.zeros((1, lg.shape[1]), F32)
        for l2 in range(1, layer + 1):
            lb = lb + sm[l2:l2 + 1, :]
        return lb

    def rows(c):
        return pl.ds(pl.multiple_of(c * CHUNK, CHUNK), CHUNK)

    lbs = (lower_bound(0), lower_bound(1))
    st_ref[...] = jnp.zeros_like(st_ref)

    z_refs, tri_refs, lvl_refs, acc_refs = (ff_ref, fb_ref), (trif_ref, trib_ref), (lvlf_ref, lvlb_ref), (of_ref, ob_ref)
    lanes = [slice(hh * LANES, (hh + 1) * LANES) for hh in range(hps)]

    def body(i, carry):
        c_b = jnp.where(i < n_ctx_chunks, n_ctx_chunks - 1 - i, n_chunks - 1 - (i - n_ctx_chunks))
        rws = (rows(i), rows(c_b))
        row = lax.broadcasted_iota(jnp.int32, (CHUNK, hps * LANES), 0)
        q, k, f, b, vb, lvl = [], [], [], [], [], []
        for d in range(2):
            fr = lbs[d] + (1.0 - lbs[d]) * _sigmoid(z_refs[d][0, rws[d], :].astype(F32))
            q.append(q_ref[0, rws[d], :].astype(F32))
            k.append(1.0 - fr)
            f.append(jnp.maximum(fr, F_FLOOR))
            b.append(_cumsum_rows(tri_refs[d][...], jnp.log(f[d])))
            vb.append(i_ref[0, rws[d], :])
            lvl.append(lvl_refs[d][...])

        chains = [(hh, d) for hh in range(hps) for d in range(2)]
        scores = {}
        for hh, d in chains:
            ls = lanes[hh]
            scores[hh, d] = jnp.where(lvl[d] == 0, _dot_nt(q[d][:, ls].astype(BF16), k[d][:, ls].astype(BF16)), 0.0)
        m = 1
        while m < CHUNK:
            w = []
            for d in range(2):
                bit = (row & m) != 0
                qside = jnp.logical_not(bit) if d == 1 else bit
                if m == 1:
                    e = jnp.where(qside, f[d], 1.0)
                else:
                    bm = _boundary_rows(b[d], m, d == 1)
                    e = jnp.exp(jnp.where(qside, b[d] - bm, bm - b[d]))
                w.append((jnp.where(qside, q[d], k[d]) * e).astype(BF16))
            for hh, d in chains:
                wh = w[d][:, lanes[hh]]
                scores[hh, d] = jnp.where(lvl[d] == m, _dot_nt(wh, wh), scores[hh, d])
            m *= 2

        qe, kdec, eb = [], [], []
        for d in range(2):
            b_end = b[d][0:1, :] if d == 1 else b[d][CHUNK - 1:CHUNK, :]
            qe.append((q[d] * jnp.exp(b[d])).astype(BF16))
            kdec.append((k[d] * jnp.exp(b_end - b[d])).astype(BF16))
            eb.append(jnp.exp(b_end))
        for hh, d in chains:
            ls = lanes[hh]
            st = st_ref[2 * hh + d]
            o = jnp.dot(scores[hh, d].astype(BF16), vb[d][:, ls], preferred_element_type=F32)
            o = o + _dot_nt(qe[d][:, ls], st.astype(BF16))
            st_ref[2 * hh + d] = st * eb[d][:, ls] + _dot_tn(vb[d][:, ls], kdec[d][:, ls])
            acc_refs[d][rws[d], ls] = o
        return carry

    lax.fori_loop(0, n_chunks, body, 0)

    nw = nw_ref[...]

    def finish(c, carry):
        r = rows(c)
        for hh in range(hps):
            ls = slice(hh * LANES, (hh + 1) * LANES)
            o = of_ref[r, ls] + ob_ref[r, ls]
            g = g_ref[0, r, ls].astype(F32)
            y = o * lax.rsqrt(jnp.mean(o * o, axis=-1, keepdims=True) + RMS_EPS) * nw * (g * _sigmoid(g))
            o_ref[0, r, ls] = y.astype(o_ref.dtype)
        return carry

    lax.fori_loop(0, n_chunks, finish, 0)


def _hgrn2(u, lb_logits, norm_w, layer, lc):
    bsz, t_tot, _ = u.shape
    depth = lb_logits.shape[1]
    hps = A_HEADS_PER_STEP
    wblk = hps * LANES
    tri_f = np.tril(np.ones((CHUNK, CHUNK), np.float32))
    lvl_f = _level_table()
    ca = COL_A // wblk
    seq = lambda off: pl.BlockSpec((1, t_tot, wblk), lambda b, h: (b, 0, ca + off * (A_HEADS // hps) + h))
    const = lambda: pl.BlockSpec((CHUNK, CHUNK), lambda b, h: (0, 0))
    kern = functools.partial(_hgrn2_kernel, layer=layer, n_ctx_chunks=lc // CHUNK)
    return pl.pallas_call(
        kern,
        out_shape=jax.ShapeDtypeStruct((bsz, t_tot, A_KW), BF16),
        grid=(bsz, A_HEADS // hps),
        in_specs=[seq(0), seq(1), seq(2), seq(3), seq(4),
                  pl.BlockSpec((2, depth, wblk), lambda b, h: (0, 0, h)),
                  pl.BlockSpec((1, LANES), lambda b, h: (0, 0)),
                  const(), const(), const(), const()],
        out_specs=pl.BlockSpec((1, t_tot, wblk), lambda b, h: (b, 0, h)),
        scratch_shapes=[pltpu.VMEM((2 * hps, A_DK, A_DK), F32),
                        pltpu.VMEM((t_tot, wblk), F32), pltpu.VMEM((t_tot, wblk), F32)],
        compiler_params=_cparams(("parallel", "parallel")),
        name="hgrn2",
    )(u, u, u, u, u, lb_logits, norm_w.reshape(1, LANES),
      jnp.asarray(tri_f, BF16), jnp.asarray(tri_f.T, BF16), jnp.asarray(lvl_f), jnp.asarray(lvl_f.T))


def _ssd_tables(gps):
    hps = gps * B_HPG
    e3 = np.zeros((2, B_HEADS // hps, LANES, hps * LANES), np.float32)
    e2 = np.zeros((2, B_HEADS // hps, LANES, hps * B_HEADDIM), np.float32)
    for d in range(2):
        for s in range(B_HEADS // hps):
            for jj in range(hps):
                lane = 16 * d + hps * s + jj
                for piece in range(3):
                    e3[d, s, 32 * piece + lane, jj * LANES:(jj + 1) * LANES] = 1.0
                for piece in range(2):
                    e2[d, s, 32 * piece + lane, jj * B_HEADDIM:(jj + 1) * B_HEADDIM] = 1.0
    return e3, e2


B_GROUPS_PER_STEP = 2
MASKED_EXPONENT = -1e30


def _ssd_kernel(z_ref, xs_ref, bm_ref, cm_ref, dt_ref, cum_ref, dsk_ref, nw_ref, e3_ref, e2_ref,
                o_ref, st_ref, yf_ref, yb_ref, rowt_ref, *, n_ctx_chunks):
    gw = B_HPG * B_HEADDIM
    gps = z_ref.shape[2] // gw
    g0 = pl.program_id(1) * gps
    n_chunks = z_ref.shape[1] // CHUNK
    c = CHUNK

    def rows(k):
        return pl.ds(pl.multiple_of(k * CHUNK, CHUNK), CHUNK)

    st_ref[...] = jnp.zeros_like(st_ref)
    accs = (yf_ref, yb_ref)

    def body(i, carry):
        c_b = jnp.where(i < n_ctx_chunks, n_ctx_chunks - 1 - i, n_chunks - 1 - (i - n_ctx_chunks))
        rws = (rows(i), rows(c_b))
        lane = lax.broadcasted_iota(jnp.int32, (c, LANES), 1)
        ti = lax.broadcasted_iota(jnp.int32, (c, c), 0)
        si = lax.broadcasted_iota(jnp.int32, (c, c), 1)
        causal = (si <= ti, si >= ti)
        half = lane < B_HEADDIM

        colb, decay_in, gain = [], [], []
        for d in range(2):
            dt = dt_ref[0, rws[d], :]
            cum = cum_ref[0, rws[d], :]
            hi, mid, lo = _split3(cum)
            cum3 = jnp.where(lane < 32, hi, jnp.where(lane < 64, mid, lo))
            colb.append(jnp.dot(cum3, e3_ref[d, 0], preferred_element_type=F32))
            rowt_ref[d] = (jnp.log(dt) - cum).T

            def expand(vals):
                vhi = vals.astype(BF16)
                vlo = (vals - vhi.astype(F32)).astype(BF16)
                return jnp.dot(jnp.where(lane < 32, vhi, vlo), e2_ref[d, 0], preferred_element_type=F32)

            cum_end = cum[0:1, :] if d == 1 else cum[c - 1:c, :]
            decay_in.append(expand(jnp.exp(cum)))
            gain.append(expand(jnp.exp(jnp.minimum(cum_end - cum, 0.0)) * dt))

        chains = [(gg, d) for gg in range(gps) for d in range(2)]
        bmb, cmb, xsf, cbs = {}, {}, {}, {}
        for gg, d in chains:
            bmb[gg, d] = bm_ref[0, rws[d], gg * LANES:(gg + 1) * LANES]
            cmb[gg, d] = cm_ref[0, rws[d], gg * LANES:(gg + 1) * LANES]
            xsf[gg, d] = xs_ref[0, rws[d], gg * gw:(gg + 1) * gw]
            cbs[gg, d] = _dot_nt(cmb[gg, d], bmb[gg, d])
        y_intra = {}
        for gg, d in chains:
            ys = []
            for pair in range(B_HPG // 2):
                xp = xsf[gg, d][:, pair * LANES:(pair + 1) * LANES]
                xblk = jnp.concatenate([jnp.where(half, xp, jnp.zeros_like(xp)),
                                        jnp.where(half, jnp.zeros_like(xp), xp)], axis=0)
                mms = []
                for sub in range(2):
                    j = 2 * pair + sub
                    ln = 16 * d + B_HPG * (g0 + gg) + j
                    col = (gg * B_HPG + j) * LANES
                    arg = colb[d][:, col:col + LANES] + rowt_ref[d, pl.ds(ln, 1), :]
                    lmat = jnp.exp(jnp.where(causal[d], arg, MASKED_EXPONENT))
                    mms.append((cbs[gg, d] * lmat).astype(BF16))
                ys.append(jnp.dot(jnp.concatenate(mms, axis=1), xblk, preferred_element_type=F32))
            y_intra[gg, d] = jnp.concatenate(ys, axis=1)
        for gg, d in chains:
            k = 2 * gg + d
            cs = slice(gg * gw, (gg + 1) * gw)
            state = st_ref[k]
            dec = decay_in[d][:, cs]
            y_inter = jnp.dot(cmb[gg, d], state.astype(BF16), preferred_element_type=F32) * dec
            upd = _dot_tn(bmb[gg, d], xsf[gg, d] * gain[d][:, cs].astype(BF16))
            dec_row = dec[0:1, :] if d == 1 else dec[c - 1:c, :]
            st_ref[k] = state * dec_row + upd
            accs[d][rws[d], cs] = y_intra[gg, d] + y_inter
        return carry

    lax.fori_loop(0, n_chunks, body, 0)

    dsk = dsk_ref[...]
    nw = nw_ref[...]

    def finish(c, carry):
        r = rows(c)
        for gg in range(gps):
            cs = slice(gg * gw, (gg + 1) * gw)
            z = z_ref[0, r, cs].astype(F32)
            y = yf_ref[r, cs] + yb_ref[r, cs] + dsk[:, cs] * xs_ref[0, r, cs].astype(F32)
            v = y * (z * _sigmoid(z))
            v = v * lax.rsqrt(jnp.mean(v * v, axis=-1, keepdims=True) + RMS_EPS) * nw[:, cs]
            o_ref[0, r, cs] = v.astype(o_ref.dtype)
        return carry

    lax.fori_loop(0, n_chunks, finish, 0)


def _ssd(u, dt, cum, d_skip, norm_w, lc):
    bsz, t_tot, _ = u.shape
    gw = B_HPG * B_HEADDIM
    gps = B_GROUPS_PER_STEP
    e3, e2 = _ssd_tables(gps)
    dsk_row = jnp.repeat(d_skip, B_HEADDIM).reshape(1, B_WIDTH)
    wblk, nblk = gps * gw, gps * LANES
    cz, cx = COL_Z // wblk, COL_XBC // wblk
    cbm = (COL_XBC + B_WIDTH) // nblk
    ccm = cbm + B_GROUPS // gps
    const2 = lambda shape: pl.BlockSpec(shape, lambda b, g: (0,) * len(shape))
    kern = functools.partial(_ssd_kernel, n_ctx_chunks=lc // CHUNK)
    return pl.pallas_call(
        kern,
        out_shape=jax.ShapeDtypeStruct((bsz, t_tot, B_WIDTH), BF16),
        grid=(bsz, B_GROUPS // gps),
        in_specs=[pl.BlockSpec((1, t_tot, wblk), lambda b, g: (b, 0, cz + g)),
                  pl.BlockSpec((1, t_tot, wblk), lambda b, g: (b, 0, cx + g)),
                  pl.BlockSpec((1, t_tot, nblk), lambda b, g: (b, 0, cbm + g)),
                  pl.BlockSpec((1, t_tot, nblk), lambda b, g: (b, 0, ccm + g)),
                  pl.BlockSpec((1, t_tot, LANES), lambda b, g: (b, 0, 0)),
                  pl.BlockSpec((1, t_tot, LANES), lambda b, g: (b, 0, 0)),
                  pl.BlockSpec((1, wblk), lambda b, g: (0, g)),
                  pl.BlockSpec((1, wblk), lambda b, g: (0, g)),
                  pl.BlockSpec((2, 1, LANES, gps * B_HPG * LANES), lambda b, g: (0, g, 0, 0)),
                  pl.BlockSpec((2, 1, LANES, wblk), lambda b, g: (0, g, 0, 0))],
        out_specs=pl.BlockSpec((1, t_tot, wblk), lambda b, g: (b, 0, g)),
        scratch_shapes=[pltpu.VMEM((2 * gps, B_STATE, gw), F32),
                        pltpu.VMEM((t_tot, wblk), F32), pltpu.VMEM((t_tot, wblk), F32),
                        pltpu.VMEM((2, LANES, CHUNK), F32)],
        compiler_params=_cparams(("parallel", "parallel")),
        name="ssd",
    )(u, u, u, u, dt, cum, dsk_row, norm_w.reshape(1, B_WIDTH), jnp.asarray(e3, BF16), jnp.asarray(e2, BF16))


def _merge_kernel(x_ref, mod_ref, ya_ref, yb_ref, ga_ref, gb_ref, wpa_ref, wpb_ref, wo_ref, lng_ref, lnb_ref,
                  o_ref, *, lc, t_off, alpha):
    t = pl.program_id(1)
    tm = x_ref.shape[1]
    pa = jnp.dot(ya_ref[0], wpa_ref[...], preferred_element_type=F32)
    pb = jnp.dot(yb_ref[0], wpb_ref[...], preferred_element_type=F32)
    merged = _sigmoid(ga_ref[0].astype(F32)) * pa + _sigmoid(gb_ref[0].astype(F32)) * pb
    out = jnp.dot(merged.astype(BF16), wo_ref[...], preferred_element_type=F32)
    tok = (t + t_off) * tm + lax.broadcasted_iota(jnp.int32, (tm, 1), 0)
    gate = jnp.where(tok < lc, mod_ref[0, 1, 2:3, :], mod_ref[0, 0, 2:3, :])
    y = alpha * x_ref[0] + gate * out
    mu = jnp.mean(y, axis=-1, keepdims=True)
    yc = y - mu
    var = jnp.mean(yc * yc, axis=-1, keepdims=True)
    o_ref[0] = yc * lax.rsqrt(var + LN_EPS) * lng_ref[...] + lnb_ref[...]


def _merge(xcat, mod, ya, yb, u, wpa, wpb, wo, ln_g, ln_b, lc, alpha, latent_only):
    bsz, t_tot, d = xcat.shape
    tm = _pick_tile(lc, (256,)) if latent_only else _pick_tile(t_tot, (768, 512, 256))
    t_off = lc // tm if latent_only else 0
    n_t = t_tot // tm - t_off
    seq = lambda w, cb=0: pl.BlockSpec((1, tm, w), lambda b, t: (b, t + t_off, cb))
    full = lambda a: pl.BlockSpec(a.shape, lambda b, t: (0,) * a.ndim, pipeline_mode=pl.Buffered(1))
    kern = functools.partial(_merge_kernel, lc=lc, t_off=t_off, alpha=alpha)
    lng = ln_g.reshape(1, d)
    lnb = ln_b.reshape(1, d)
    return pl.pallas_call(
        kern,
        out_shape=jax.ShapeDtypeStruct((bsz, n_t * tm, d), F32),
        grid=(bsz, n_t),
        in_specs=[seq(d), pl.BlockSpec((1, 2, 3, d), lambda b, t: (b, 0, 0, 0)),
                  seq(A_KW), seq(B_WIDTH), seq(d, 0), seq(d, 1),
                  full(wpa), full(wpb), full(wo), full(lng), full(lnb)],
        out_specs=pl.BlockSpec((1, tm, d), lambda b, t: (b, t, 0)),
        compiler_params=_cparams(("parallel", "parallel")),
        name="merge",
    )(xcat, mod, ya, yb, u, u, wpa, wpb, wo, lng, lnb)


def kernel(x, c, ctx, c_ctx, w_mod, b_mod, w_in, a_lb_logits, a_norm_w, b_conv_w, b_conv_b, b_dt_bias, b_a_log,
           b_d, b_norm_w, w_proj_a, w_proj_b, w_out, ln_g, ln_b):
    bsz, seq, d = x.shape
    lc = ctx.shape[1]
    depth = w_mod.shape[0]
    assert 2 * d == COL_A - COL_GATE and d == 2 * A_KW and seq % GRID_W == 0
    assert lc % 256 == 0 and (lc + seq) % 256 == 0 and seq % CHUNK == 0
    alpha = (2 * depth) ** 0.25

    rows = -(-(bsz + 1) // 8) * 8
    cc = jnp.zeros((rows, d), F32).at[:bsz].set(c).at[bsz].set(c_ctx)
    mod_all = _modulation(cc, w_mod, b_mod)
    ml = mod_all[:, :bsz].reshape(depth, bsz, 1, 3, d)
    mc = jnp.broadcast_to(mod_all[:, bsz].reshape(depth, 1, 1, 3, d), (depth, bsz, 1, 3, d))
    mod_all = jnp.concatenate([ml, mc], axis=2)

    o_dt = 5 * A_KW + B_WIDTH + (B_WIDTH + 2 * B_GROUPS * B_STATE)
    o_gate = o_dt + 2 * B_HEADS
    w_u = jnp.concatenate([w_in[:, :, o_gate:], w_in[:, :, :o_dt]], axis=2).astype(BF16)
    w_dt = jnp.tile(w_in[:, :, o_dt:o_gate], (1, 1, LANES // (2 * B_HEADS))).astype(BF16)
    wpa = w_proj_a.astype(BF16)
    wpb = w_proj_b.astype(BF16)
    wo = w_out.astype(BF16)

    xcat = jnp.concatenate([ctx, x], axis=1)
    for l in range(depth):
        mod = mod_all[l]
        u, dt, cum = _inproj(xcat, mod, w_u[l], w_dt[l], b_conv_w[l], b_conv_b[l].reshape(1, -1),
                             b_dt_bias[l], b_a_log[l], lc)
        ya = _hgrn2(u, a_lb_logits, a_norm_w[l], l, lc)
        yb = _ssd(u, dt, cum, b_d[l], b_norm_w[l], lc)
        xcat = _merge(xcat, mod, ya, yb, u, wpa[l], wpb[l], wo[l], ln_g[l], ln_b[l], lc, alpha,
                      latent_only=(l == depth - 1))
    return xcat
```

```python
import functools

import numpy as np
import jax
import jax.numpy as jnp
from jax import lax
from jax.experimental import pallas as pl
from jax.experimental.pallas import tpu as pltpu

F32 = jnp.float32
BF16 = jnp.bfloat16

A_HEADS = 4
A_DK = 128
A_KW = A_HEADS * A_DK
B_WIDTH = 1024
B_HEADDIM = 64
B_HEADS = B_WIDTH // B_HEADDIM
B_GROUPS = 4
B_HPG = B_HEADS // B_GROUPS
B_STATE = 128
B_CONV = 5
GRID_W = 64
LN_EPS = 1e-5
RMS_EPS = 1e-6
F_FLOOR = 1e-30
LOG2E = 1.4426950408889634

CHUNK = 128
LANES = 128
VMEM_LIMIT = 56 * 1024 * 1024

B_CONV_CH = B_WIDTH + 2 * B_GROUPS * B_STATE
COL_GATE = 0
COL_Z = 2048
COL_XBC = COL_Z + B_WIDTH
COL_A = COL_XBC + B_CONV_CH
N_U = COL_A + 5 * A_KW
W_COL_A = 0
W_COL_Z = 5 * A_KW
W_COL_DT = W_COL_Z + B_WIDTH + B_CONV_CH
W_COL_GATE = W_COL_DT + 2 * B_HEADS


def _sigmoid(x):
    return 0.5 * jnp.tanh(0.5 * x) + 0.5


def _silu(x):
    h = 0.5 * x
    return h + h * jnp.tanh(h)


def _softplus(x):
    return jnp.maximum(x, 0.0) + jnp.log(1.0 + jnp.exp(-jnp.abs(x)))


def _split3(x):
    hi = x.astype(BF16)
    r = x - hi.astype(F32)
    mid = r.astype(BF16)
    lo = (r - mid.astype(F32)).astype(BF16)
    return hi, mid, lo


def _cumsum_rows(tri, x):
    hi, mid, lo = _split3(x)
    d = lambda a: jnp.dot(tri, a, preferred_element_type=F32)
    return d(hi) + d(mid) + d(lo)


def _dot_nt(a, b):
    return lax.dot_general(a, b, (((1,), (1,)), ((), ())), preferred_element_type=F32)


def _dot_tn(a, b):
    return lax.dot_general(a, b, (((0,), (0,)), ((), ())), preferred_element_type=F32)


def _cparams(sem):
    return pltpu.CompilerParams(dimension_semantics=sem, vmem_limit_bytes=VMEM_LIMIT)


def _mod_kernel(c_ref, w_ref, b_ref, o_ref):
    a = c_ref[...]
    a = _silu(a)
    o_ref[0] = jnp.dot(a, w_ref[0], preferred_element_type=F32) + b_ref[0]


def _modulation(cc, w_mod, b_mod):
    depth, d, d3 = w_mod.shape
    rows = cc.shape[0]
    tn = 1024
    return pl.pallas_call(
        _mod_kernel,
        out_shape=jax.ShapeDtypeStruct((depth, rows, d3), F32),
        grid=(depth, d3 // tn),
        in_specs=[pl.BlockSpec((rows, d), lambda l, j: (0, 0)),
                  pl.BlockSpec((1, d, tn), lambda l, j: (l, 0, j)),
                  pl.BlockSpec((1, 1, tn), lambda l, j: (l, 0, j))],
        out_specs=pl.BlockSpec((1, rows, tn), lambda l, j: (l, 0, j)),
        compiler_params=_cparams(("arbitrary", "arbitrary")),
        name="modulation",
    )(cc, w_mod, b_mod.reshape(depth, 1, d3))


def _modulate(x, mod_ref, tok0, lc):
    tm = x.shape[0]
    tok = tok0 + lax.broadcasted_iota(jnp.int32, (tm, 1), 0)
    is_ctx = tok < lc
    shift = jnp.where(is_ctx, mod_ref[0, 1, 0:1, :], mod_ref[0, 0, 0:1, :])
    scale = jnp.where(is_ctx, mod_ref[0, 1, 1:2, :], mod_ref[0, 0, 1:2, :])
    return x * (1.0 + scale) + shift


def _shift_vreg_rows(rot, ctx_first, n_ctx_v, down):
    nv = rot.shape[0]
    vpr = GRID_W // 8
    zero = jnp.zeros_like(rot[0:1])
    pieces = []
    for g0 in range(0, nv, vpr):
        g1 = g0 + vpr
        if down:
            edge = jnp.where(ctx_first, rot[g0 - 1:g0], 0.0) if 0 < g0 < n_ctx_v else zero
            pieces += [edge, rot[g0:g1 - 1]]
        else:
            edge = jnp.where(ctx_first, rot[g1:g1 + 1], 0.0) if g1 < n_ctx_v else zero
            pieces += [rot[g0 + 1:g1], edge]
    return jnp.concatenate(pieces, axis=0)


def _conv_rows(acc, cw, cb, ctx_first, n_ctx_v):
    n, tn = acc.shape
    a3 = acc.reshape(n // 8, 8, tn)
    sub = lax.broadcasted_iota(jnp.int32, a3.shape, 1)
    pad = B_CONV // 2
    out = a3 * cw[pad:pad + 1, :] + cb
    for d in range(1, pad + 1):
        rot = pltpu.roll(a3, d, axis=1)
        prev = _shift_vreg_rows(rot, ctx_first, n_ctx_v, True)
        out = out + jnp.where(sub >= d, rot, prev) * cw[pad - d:pad - d + 1, :]
        rot = pltpu.roll(a3, 8 - d, axis=1)
        nxt = _shift_vreg_rows(rot, ctx_first, n_ctx_v, False)
        out = out + jnp.where(sub < 8 - d, rot, nxt) * cw[pad + d:pad + d + 1, :]
    return out.reshape(n, tn)


INPROJ_TN = 256


def _inproj_kernel(x_ref, mod_ref, wm_ref, wg_ref, wdt_ref, cw_ref, cb_ref, bias_ref, aneg_ref, tril_ref, triu_ref,
                   u_ref, dt_ref, cum_ref, h_ref, *, lc):
    t = pl.program_id(1)
    tm = x_ref.shape[1]
    h_ref[...] = _modulate(x_ref[0], mod_ref, t * tm, lc).astype(BF16)
    plain = [j for j in range(0, N_U, INPROJ_TN) if not COL_XBC <= j < COL_A]
    conv = list(range(COL_XBC, COL_A, INPROJ_TN))
    order = []
    while plain or conv:
        if conv:
            order.append(conv.pop(0))
        order += plain[:2]
        plain = plain[2:]
    for j0 in order:
        j1 = j0 + INPROJ_TN
        if j0 < COL_Z:
            w = wg_ref[:, j0 - COL_GATE:j1 - COL_GATE]
        elif j0 < COL_A:
            w = wm_ref[:, j0 - COL_Z + W_COL_Z:j1 - COL_Z + W_COL_Z]
        else:
            w = wm_ref[:, j0 - COL_A + W_COL_A:j1 - COL_A + W_COL_A]
        acc = jnp.dot(h_ref[...], w, preferred_element_type=F32)
        if COL_XBC <= j0 < COL_A:
            acc = _conv_rows(acc, cw_ref[:, j0 - COL_XBC:j1 - COL_XBC], cb_ref[:, j0 - COL_XBC:j1 - COL_XBC],
                             t == 0, lc // 8)
            acc = _silu(acc)
        u_ref[0, :, j0:j1] = acc.astype(u_ref.dtype)

    dt = _softplus(jnp.dot(h_ref[...], wdt_ref[...], preferred_element_type=F32) + bias_ref[...])
    dt_ref[0] = dt
    da = dt * aneg_ref[...]
    fwd_lane = (lax.broadcasted_iota(jnp.int32, (CHUNK, LANES), 1) & B_HEADS) == 0
    for r0 in range(0, tm, CHUNK):
        blk = da[r0:r0 + CHUNK]
        cum_ref[0, r0:r0 + CHUNK, :] = jnp.where(fwd_lane, _cumsum_rows(tril_ref[...], blk),
                                                 _cumsum_rows(triu_ref[...], blk))


def _pick_tile(total, candidates):
    for c in candidates:
        if total % c == 0:
            return c
    raise ValueError(f"no tile for {total}")


def _inproj(xcat, mod, w_all, layer, w_gate, w_dt, conv_w, conv_b, dt_bias, a_log, lc):
    bsz, t_tot, d = xcat.shape
    tm = _pick_tile(t_tot, (768, 512, 256))
    assert lc <= tm and all(c % INPROJ_TN == 0 for c in (COL_Z, COL_XBC, COL_A, N_U, W_COL_Z, W_COL_DT))
    rep = LANES // (2 * B_HEADS)
    bias_row = jnp.tile(dt_bias.reshape(1, 2 * B_HEADS), (1, rep))
    aneg_row = -jnp.exp(jnp.tile(a_log.reshape(1, 2 * B_HEADS), (1, rep)).astype(F32))
    tril = np.tril(np.ones((CHUNK, CHUNK), np.float32))
    full = lambda shape: pl.BlockSpec(shape, lambda b, t: (0,) * len(shape))
    resident = lambda shape: pl.BlockSpec(shape, lambda b, t: (0,) * len(shape), pipeline_mode=pl.Buffered(1))
    seq = lambda w: pl.BlockSpec((1, tm, w), lambda b, t: (b, t, 0))
    return pl.pallas_call(
        functools.partial(_inproj_kernel, lc=lc),
        out_shape=(jax.ShapeDtypeStruct((bsz, t_tot, N_U), BF16),
                   jax.ShapeDtypeStruct((bsz, t_tot, LANES), F32), jax.ShapeDtypeStruct((bsz, t_tot, LANES), F32)),
        grid=(bsz, t_tot // tm),
        in_specs=[seq(d), pl.BlockSpec((1, 2, 3, d), lambda b, t: (b, 0, 0, 0)),
                  pl.BlockSpec((None, d, W_COL_DT), lambda b, t: (layer, 0, 0), pipeline_mode=pl.Buffered(1)),
                  resident((d, 2 * d)), resident((d, LANES)),
                  full((B_CONV, B_CONV_CH)), full((1, B_CONV_CH)),
                  full((1, LANES)), full((1, LANES)), full((CHUNK, CHUNK)), full((CHUNK, CHUNK))],
        out_specs=(seq(N_U), seq(LANES), seq(LANES)),
        scratch_shapes=[pltpu.VMEM((tm, d), BF16)],
        compiler_params=_cparams(("parallel", "parallel")),
        name="inproj",
    )(xcat, mod, w_all, w_gate, w_dt, conv_w, conv_b, bias_row, aneg_row,
      jnp.asarray(tril, BF16), jnp.asarray(tril.T, BF16))


def _level_table():
    t = np.arange(CHUNK)[:, None]
    s = np.arange(CHUNK)[None, :]
    x = t ^ s
    lvl = np.where(x > 0, 2 ** np.floor(np.log2(np.maximum(x, 1))).astype(np.int64), 0)
    return np.where(s <= t, lvl, -1).astype(np.int32)


def _boundary_rows(b, m, backward):
    n = b.shape[0]
    off = m if backward else m - 1
    if m >= 8:
        parts = [jnp.broadcast_to(b[g0 + off:g0 + off + 1, :], (2 * m, b.shape[1])) for g0 in range(0, n, 2 * m)]
        return parts[0] if len(parts) == 1 else jnp.concatenate(parts, axis=0)
    b3 = b.reshape(n // 8, 8, b.shape[1])
    sub = lax.broadcasted_iota(jnp.int32, b3.shape, 1)
    out = None
    for g0 in range(0, 8, 2 * m):
        src = jnp.broadcast_to(b3[:, g0 + off:g0 + off + 1, :], b3.shape)
        out = src if out is None else jnp.where(sub >= g0, src, out)
    return out.reshape(b.shape)


A_HEADS_PER_STEP = 4
A_HEADS_PER_STAGE = 2


def _hgrn2_kernel(q_ref, ff_ref, fb_ref, i_ref, lbl_ref, trif_ref, trib_ref, lvlf_ref, lvlb_ref,
                  of_ref, ob_ref, st_ref, *, layer, n_ctx_chunks):
    n_chunks = q_ref.shape[1] // CHUNK
    hps = q_ref.shape[2] // LANES

    def lower_bound(d):
        lg = lbl_ref[d]
        e = jnp.exp(lg - jnp.max(lg, axis=0, keepdims=True))
        sm = e / jnp.sum(e, axis=0, keepdims=True)
        lb = jnp.zeros((1, lg.shape[1]), F32)
        for l2 in range(1, layer + 1):
            lb = lb + sm[l2:l2 + 1, :]
        return lb

    def rows(c):
        return pl.ds(pl.multiple_of(c * CHUNK, CHUNK), CHUNK)

    lbs = (lower_bound(0), lower_bound(1))
    st_ref[...] = jnp.zeros_like(st_ref)

    z_refs, tri_refs, lvl_refs, acc_refs = (ff_ref, fb_ref), (trif_ref, trib_ref), (lvlf_ref, lvlb_ref), (of_ref, ob_ref)
    lanes = [slice(hh * LANES, (hh + 1) * LANES) for hh in range(hps)]

    def body(i, carry):
        c_b = jnp.where(i < n_ctx_chunks, n_ctx_chunks - 1 - i, n_chunks - 1 - (i - n_ctx_chunks))
        rws = (rows(i), rows(c_b))
        for h0 in range(0, hps, A_HEADS_PER_STAGE):
            heads = list(range(h0, min(h0 + A_HEADS_PER_STAGE, hps)))
            gl = slice(heads[0] * LANES, (heads[-1] + 1) * LANES)
            loc = {hh: slice((hh - h0) * LANES, (hh - h0 + 1) * LANES) for hh in heads}
            row = lax.broadcasted_iota(jnp.int32, (CHUNK, len(heads) * LANES), 0)
            q, k, f, b, vb, lvl = [], [], [], [], [], []
            for d in range(2):
                lb = lbs[d][:, gl]
                fr = lb + (1.0 - lb) * _sigmoid(z_refs[d][0, rws[d], gl].astype(F32))
                q.append(q_ref[0, rws[d], gl].astype(F32))
                k.append(1.0 - fr)
                f.append(jnp.maximum(fr, F_FLOOR))
                b.append(_cumsum_rows(tri_refs[d][...], jnp.log(f[d]) * LOG2E))
                vb.append(i_ref[0, rws[d], gl])
                lvl.append(lvl_refs[d][...])

            chains = [(hh, d) for hh in heads for d in range(2)]
            scores = {}
            for hh, d in chains:
                ls = loc[hh]
                scores[hh, d] = jnp.where(lvl[d] == 0,
                                          _dot_nt(q[d][:, ls].astype(BF16), k[d][:, ls].astype(BF16)), 0.0)
            m = 1
            while m < CHUNK:
                w = []
                for d in range(2):
                    bit = (row & m) != 0
                    qside = jnp.logical_not(bit) if d == 1 else bit
                    if m == 1:
                        e = jnp.where(qside, f[d], 1.0)
                    else:
                        bm = _boundary_rows(b[d], m, d == 1)
                        e = jnp.exp2(jnp.where(qside, b[d] - bm, bm - b[d]))
                    w.append((jnp.where(qside, q[d], k[d]) * e).astype(BF16))
                for hh, d in chains:
                    wh = w[d][:, loc[hh]]
                    scores[hh, d] = jnp.where(lvl[d] == m, _dot_nt(wh, wh), scores[hh, d])
                m *= 2

            qe, kdec, eb = [], [], []
            for d in range(2):
                b_end = b[d][0:1, :] if d == 1 else b[d][CHUNK - 1:CHUNK, :]
                qe.append((q[d] * jnp.exp2(b[d])).astype(BF16))
                kdec.append((k[d] * jnp.exp2(b_end - b[d])).astype(BF16))
                eb.append(jnp.exp2(b_end))
            for hh, d in chains:
                ls = loc[hh]
                st = st_ref[2 * hh + d]
                o = jnp.dot(scores[hh, d].astype(BF16), vb[d][:, ls], preferred_element_type=F32)
                o = o + _dot_nt(qe[d][:, ls], st.astype(BF16))
                st_ref[2 * hh + d] = st * eb[d][:, ls] + _dot_tn(vb[d][:, ls], kdec[d][:, ls])
                acc_refs[d][0, rws[d], lanes[hh]] = o.astype(acc_refs[d].dtype)
        return carry

    lax.fori_loop(0, n_chunks, body, 0)


def _hgrn2(u, lb_logits, layer, lc):
    bsz, t_tot, _ = u.shape
    depth = lb_logits.shape[1]
    hps = A_HEADS_PER_STEP
    wblk = hps * LANES
    tri_f = np.tril(np.ones((CHUNK, CHUNK), np.float32))
    lvl_f = _level_table()
    ca = COL_A // wblk
    seq = lambda off: pl.BlockSpec((1, t_tot, wblk), lambda b, h: (b, 0, ca + off * (A_HEADS // hps) + h))
    const = lambda: pl.BlockSpec((CHUNK, CHUNK), lambda b, h: (0, 0))
    kern = functools.partial(_hgrn2_kernel, layer=layer, n_ctx_chunks=lc // CHUNK)
    out = jax.ShapeDtypeStruct((bsz, t_tot, A_KW), BF16)
    out_spec = pl.BlockSpec((1, t_tot, wblk), lambda b, h: (b, 0, h))
    return pl.pallas_call(
        kern,
        out_shape=(out, out),
        grid=(bsz, A_HEADS // hps),
        in_specs=[seq(0), seq(1), seq(2), seq(3),
                  pl.BlockSpec((2, depth, wblk), lambda b, h: (0, 0, h)),
                  const(), const(), const(), const()],
        out_specs=(out_spec, out_spec),
        scratch_shapes=[pltpu.VMEM((2 * hps, A_DK, A_DK), F32)],
        compiler_params=_cparams(("parallel", "parallel")),
        name="hgrn2",
    )(u, u, u, u, lb_logits,
      jnp.asarray(tri_f, BF16), jnp.asarray(tri_f.T, BF16), jnp.asarray(lvl_f), jnp.asarray(lvl_f.T))


def _ssd_tables(gps):
    hps = gps * B_HPG
    e3 = np.zeros((2, B_HEADS // hps, LANES, hps * LANES), np.float32)
    e2 = np.zeros((2, B_HEADS // hps, LANES, hps * B_HEADDIM), np.float32)
    for d in range(2):
        for s in range(B_HEADS // hps):
            for jj in range(hps):
                lane = 16 * d + hps * s + jj
                for piece in range(3):
                    e3[d, s, 32 * piece + lane, jj * LANES:(jj + 1) * LANES] = 1.0
                for piece in range(2):
                    e2[d, s, 32 * piece + lane, jj * B_HEADDIM:(jj + 1) * B_HEADDIM] = 1.0
    return e3, e2


B_GROUPS_PER_STEP = 2
MASKED_EXPONENT = -1e30


def _ssd_kernel(xs_ref, bm_ref, cm_ref, dt_ref, cum_ref, e3_ref, e2_ref,
                yf_ref, yb_ref, st_ref, rowt_ref, *, n_ctx_chunks):
    gw = B_HPG * B_HEADDIM
    gps = xs_ref.shape[2] // gw
    g0 = pl.program_id(1) * gps
    n_chunks = xs_ref.shape[1] // CHUNK
    c = CHUNK

    def rows(k):
        return pl.ds(pl.multiple_of(k * CHUNK, CHUNK), CHUNK)

    st_ref[...] = jnp.zeros_like(st_ref)
    accs = (yf_ref, yb_ref)

    def body(i, carry):
        c_b = jnp.where(i < n_ctx_chunks, n_ctx_chunks - 1 - i, n_chunks - 1 - (i - n_ctx_chunks))
        rws = (rows(i), rows(c_b))
        lane = lax.broadcasted_iota(jnp.int32, (c, LANES), 1)
        ti = lax.broadcasted_iota(jnp.int32, (c, c), 0)
        si = lax.broadcasted_iota(jnp.int32, (c, c), 1)
        causal = (si <= ti, si >= ti)
        half = lane < B_HEADDIM

        colb, decay_in, gain = [], [], []
        for d in range(2):
            dt = dt_ref[0, rws[d], :]
            cum = cum_ref[0, rws[d], :]
            hi, mid, lo = _split3(cum)
            cum3 = jnp.where(lane < 32, hi, jnp.where(lane < 64, mid, lo))
            colb.append(jnp.dot(cum3, e3_ref[d, 0], preferred_element_type=F32))
            rowt_ref[d] = (jnp.log(dt) - cum).T

            def expand(vals):
                vhi = vals.astype(BF16)
                vlo = (vals - vhi.astype(F32)).astype(BF16)
                return jnp.dot(jnp.where(lane < 32, vhi, vlo), e2_ref[d, 0], preferred_element_type=F32)

            cum_end = cum[0:1, :] if d == 1 else cum[c - 1:c, :]
            decay_in.append(expand(jnp.exp(cum)))
            gain.append(expand(jnp.exp(jnp.minimum(cum_end - cum, 0.0)) * dt))

        chains = [(gg, d) for gg in range(gps) for d in range(2)]
        bmb, cmb, xsf, cbs = {}, {}, {}, {}
        for gg, d in chains:
            bmb[gg, d] = bm_ref[0, rws[d], gg * LANES:(gg + 1) * LANES]
            cmb[gg, d] = cm_ref[0, rws[d], gg * LANES:(gg + 1) * LANES]
            xsf[gg, d] = xs_ref[0, rws[d], gg * gw:(gg + 1) * gw]
            cbs[gg, d] = _dot_nt(cmb[gg, d], bmb[gg, d])
        y_intra = {}
        for gg, d in chains:
            ys = []
            for pair in range(B_HPG // 2):
                xp = xsf[gg, d][:, pair * LANES:(pair + 1) * LANES]
                xblk = jnp.concatenate([jnp.where(half, xp, jnp.zeros_like(xp)),
                                        jnp.where(half, jnp.zeros_like(xp), xp)], axis=0)
                mms = []
                for sub in range(2):
                    j = 2 * pair + sub
                    ln = 16 * d + B_HPG * (g0 + gg) + j
                    col = (gg * B_HPG + j) * LANES
                    arg = colb[d][:, col:col + LANES] + rowt_ref[d, pl.ds(ln, 1), :]
                    lmat = jnp.exp(jnp.where(causal[d], arg, MASKED_EXPONENT))
                    mms.append((cbs[gg, d] * lmat).astype(BF16))
                ys.append(jnp.dot(jnp.concatenate(mms, axis=1), xblk, preferred_element_type=F32))
            y_intra[gg, d] = jnp.concatenate(ys, axis=1)
        for gg, d in chains:
            k = 2 * gg + d
            cs = slice(gg * gw, (gg + 1) * gw)
            state = st_ref[k]
            dec = decay_in[d][:, cs]
            y_inter = jnp.dot(cmb[gg, d], state.astype(BF16), preferred_element_type=F32) * dec
            upd = _dot_tn(bmb[gg, d], xsf[gg, d] * gain[d][:, cs].astype(BF16))
            dec_row = dec[0:1, :] if d == 1 else dec[c - 1:c, :]
            st_ref[k] = state * dec_row + upd
            accs[d][0, rws[d], cs] = (y_intra[gg, d] + y_inter).astype(accs[d].dtype)
        return carry

    lax.fori_loop(0, n_chunks, body, 0)


def _ssd(u, dt, cum, lc):
    bsz, t_tot, _ = u.shape
    gw = B_HPG * B_HEADDIM
    gps = B_GROUPS_PER_STEP
    e3, e2 = _ssd_tables(gps)
    wblk, nblk = gps * gw, gps * LANES
    cx = COL_XBC // wblk
    cbm = (COL_XBC + B_WIDTH) // nblk
    ccm = cbm + B_GROUPS // gps
    kern = functools.partial(_ssd_kernel, n_ctx_chunks=lc // CHUNK)
    out = jax.ShapeDtypeStruct((bsz, t_tot, B_WIDTH), BF16)
    out_spec = pl.BlockSpec((1, t_tot, wblk), lambda b, g: (b, 0, g))
    return pl.pallas_call(
        kern,
        out_shape=(out, out),
        grid=(bsz, B_GROUPS // gps),
        in_specs=[pl.BlockSpec((1, t_tot, wblk), lambda b, g: (b, 0, cx + g)),
                  pl.BlockSpec((1, t_tot, nblk), lambda b, g: (b, 0, cbm + g)),
                  pl.BlockSpec((1, t_tot, nblk), lambda b, g: (b, 0, ccm + g)),
                  pl.BlockSpec((1, t_tot, LANES), lambda b, g: (b, 0, 0)),
                  pl.BlockSpec((1, t_tot, LANES), lambda b, g: (b, 0, 0)),
                  pl.BlockSpec((2, 1, LANES, gps * B_HPG * LANES), lambda b, g: (0, g, 0, 0)),
                  pl.BlockSpec((2, 1, LANES, wblk), lambda b, g: (0, g, 0, 0))],
        out_specs=(out_spec, out_spec),
        scratch_shapes=[pltpu.VMEM((2 * gps, B_STATE, gw), F32), pltpu.VMEM((2, LANES, CHUNK), F32)],
        compiler_params=_cparams(("parallel", "parallel")),
        name="ssd",
    )(u, u, u, dt, cum, jnp.asarray(e3, BF16), jnp.asarray(e2, BF16))


def _group_rms(v, width):
    parts = []
    for c0 in range(0, v.shape[1], width):
        p = v[:, c0:c0 + width]
        parts.append(p * lax.rsqrt(jnp.mean(p * p, axis=-1, keepdims=True) + RMS_EPS))
    return jnp.concatenate(parts, axis=1)


def _merge_kernel(x_ref, mod_ref, of_ref, ob_ref, g_ref, yf_ref, yb_ref, xs_ref, z_ref, ga_ref, gb_ref,
                  anw_ref, dsk_ref, bnw_ref, wpa_ref, wpb_ref, wo_ref, lng_ref, lnb_ref, o_ref, *, lc, t_off, alpha):
    t = pl.program_id(1)
    tm = x_ref.shape[1]
    oa = of_ref[0].astype(F32) + ob_ref[0].astype(F32)
    ya = _group_rms(oa, A_DK) * anw_ref[...] * _silu(g_ref[0].astype(F32))
    yb = yf_ref[0].astype(F32) + yb_ref[0].astype(F32) + dsk_ref[...] * xs_ref[0].astype(F32)
    yb = _group_rms(yb * _silu(z_ref[0].astype(F32)), B_WIDTH // B_GROUPS) * bnw_ref[...]
    pa = jnp.dot(ya.astype(BF16), wpa_ref[...], preferred_element_type=F32)
    pb = jnp.dot(yb.astype(BF16), wpb_ref[...], preferred_element_type=F32)
    merged = _sigmoid(ga_ref[0].astype(F32)) * pa + _sigmoid(gb_ref[0].astype(F32)) * pb
    out = jnp.dot(merged.astype(BF16), wo_ref[...], preferred_element_type=F32)
    tok = (t + t_off) * tm + lax.broadcasted_iota(jnp.int32, (tm, 1), 0)
    gate = jnp.where(tok < lc, mod_ref[0, 1, 2:3, :], mod_ref[0, 0, 2:3, :])
    y = alpha * x_ref[0] + gate * out
    mu = jnp.mean(y, axis=-1, keepdims=True)
    yc = y - mu
    var = jnp.mean(yc * yc, axis=-1, keepdims=True)
    o_ref[0] = yc * lax.rsqrt(var + LN_EPS) * lng_ref[...] + lnb_ref[...]


def _merge(xcat, mod, oa, yb, u, a_norm_w, d_skip, b_norm_w, wpa, wpb, wo, ln_g, ln_b, lc, alpha, latent_only):
    bsz, t_tot, d = xcat.shape
    tm = _pick_tile(lc, (256,)) if latent_only else _pick_tile(t_tot, (768, 512, 256))
    t_off = lc // tm if latent_only else 0
    n_t = t_tot // tm - t_off
    seq = lambda w, col=0: pl.BlockSpec((1, tm, w), lambda b, t: (b, t + t_off, col // w))
    full = lambda a: pl.BlockSpec(a.shape, lambda b, t: (0,) * a.ndim, pipeline_mode=pl.Buffered(1))
    kern = functools.partial(_merge_kernel, lc=lc, t_off=t_off, alpha=alpha)
    lng = ln_g.reshape(1, d)
    lnb = ln_b.reshape(1, d)
    anw = jnp.tile(a_norm_w.reshape(1, A_DK), (1, A_HEADS))
    dsk = jnp.repeat(d_skip, B_HEADDIM).reshape(1, B_WIDTH)
    bnw = b_norm_w.reshape(1, B_WIDTH)
    return pl.pallas_call(
        kern,
        out_shape=jax.ShapeDtypeStruct((bsz, n_t * tm, d), F32),
        grid=(bsz, n_t),
        in_specs=[seq(d), pl.BlockSpec((1, 2, 3, d), lambda b, t: (b, 0, 0, 0)),
                  seq(A_KW), seq(A_KW), seq(A_KW, COL_A + 4 * A_KW),
                  seq(B_WIDTH), seq(B_WIDTH), seq(B_WIDTH, COL_XBC), seq(B_WIDTH, COL_Z),
                  seq(d, COL_GATE), seq(d, COL_GATE + d),
                  full(anw), full(dsk), full(bnw), full(wpa), full(wpb), full(wo), full(lng), full(lnb)],
        out_specs=pl.BlockSpec((1, tm, d), lambda b, t: (b, t, 0)),
        compiler_params=_cparams(("parallel", "parallel")),
        name="merge",
    )(xcat, mod, oa[0], oa[1], u, yb[0], yb[1], u, u, u, u, anw, dsk, bnw, wpa, wpb, wo, lng, lnb)


def kernel(x, c, ctx, c_ctx, w_mod, b_mod, w_in, a_lb_logits, a_norm_w, b_conv_w, b_conv_b, b_dt_bias, b_a_log,
           b_d, b_norm_w, w_proj_a, w_proj_b, w_out, ln_g, ln_b):
    bsz, seq, d = x.shape
    lc = ctx.shape[1]
    depth = w_mod.shape[0]
    assert 2 * d == COL_Z - COL_GATE and d == 2 * A_KW and d == B_WIDTH and seq % GRID_W == 0
    assert lc % 256 == 0 and (lc + seq) % 256 == 0 and seq % CHUNK == 0
    assert w_in.shape[2] == W_COL_GATE + 2 * d
    alpha = (2 * depth) ** 0.25

    rows = -(-(bsz + 1) // 8) * 8
    cc = jnp.zeros((rows, d), F32).at[:bsz].set(c).at[bsz].set(c_ctx)
    mod_all = _modulation(cc, w_mod, b_mod)
    ml = mod_all[:, :bsz].reshape(depth, bsz, 1, 3, d)
    mc = jnp.broadcast_to(mod_all[:, bsz].reshape(depth, 1, 1, 3, d), (depth, bsz, 1, 3, d))
    mod_all = jnp.concatenate([ml, mc], axis=2)

    w_all = w_in.astype(BF16)
    w_gate = w_all[:, :, W_COL_GATE:]
    w_dt = jnp.tile(w_all[:, :, W_COL_DT:W_COL_GATE], (1, 1, LANES // (2 * B_HEADS)))
    wpa = w_proj_a.astype(BF16)
    wpb = w_proj_b.astype(BF16)
    wo = w_out.astype(BF16)

    xcat = jnp.concatenate([ctx, x], axis=1)
    for l in range(depth):
        mod = mod_all[l]
        u, dt, cum = _inproj(xcat, mod, w_all, l, w_gate[l], w_dt[l], b_conv_w[l], b_conv_b[l].reshape(1, -1),
                             b_dt_bias[l], b_a_log[l], lc)
        oa = _hgrn2(u, a_lb_logits, l, lc)
        yb = _ssd(u, dt, cum, lc)
        xcat = _merge(xcat, mod, oa, yb, u, a_norm_w[l], b_d[l], b_norm_w[l], wpa[l], wpb[l], wo[l],
                      ln_g[l], ln_b[l], lc, alpha, latent_only=(l == depth - 1))
    return xcat
```

```python
import functools

import numpy as np
import jax
import jax.numpy as jnp
from jax import lax
from jax.experimental import pallas as pl
from jax.experimental.pallas import tpu as pltpu

F32 = jnp.float32
BF16 = jnp.bfloat16

A_HEADS = 4
A_DK = 128
A_KW = A_HEADS * A_DK
B_WIDTH = 1024
B_HEADDIM = 64
B_HEADS = B_WIDTH // B_HEADDIM
B_GROUPS = 4
B_HPG = B_HEADS // B_GROUPS
B_STATE = 128
B_CONV = 5
GRID_W = 64
LN_EPS = 1e-5
RMS_EPS = 1e-6
F_FLOOR = 1e-30
LOG2E = 1.4426950408889634

CHUNK = 128
LANES = 128
VMEM_LIMIT = 56 * 1024 * 1024

B_CONV_CH = B_WIDTH + 2 * B_GROUPS * B_STATE
COL_GATE = 0
COL_Z = 2048
COL_XBC = COL_Z + B_WIDTH
COL_A = COL_XBC + B_CONV_CH
N_U = COL_A + 5 * A_KW
W_COL_A = 0
W_COL_Z = 5 * A_KW
W_COL_DT = W_COL_Z + B_WIDTH + B_CONV_CH
W_COL_GATE = W_COL_DT + 2 * B_HEADS


def _sigmoid(x):
    return 0.5 * jnp.tanh(0.5 * x) + 0.5


def _silu(x):
    h = 0.5 * x
    return h + h * jnp.tanh(h)


def _softplus(x):
    return jnp.maximum(x, 0.0) + jnp.log(1.0 + jnp.exp(-jnp.abs(x)))


def _split3(x):
    hi = x.astype(BF16)
    r = x - hi.astype(F32)
    mid = r.astype(BF16)
    lo = (r - mid.astype(F32)).astype(BF16)
    return hi, mid, lo


def _cumsum_rows(tri, x, pieces=3):
    d = lambda a: jnp.dot(tri, a, preferred_element_type=F32)
    if pieces == 2:
        hi = x.astype(BF16)
        return d(hi) + d((x - hi.astype(F32)).astype(BF16))
    hi, mid, lo = _split3(x)
    return d(hi) + d(mid) + d(lo)


def _dot_nt(a, b):
    return lax.dot_general(a, b, (((1,), (1,)), ((), ())), preferred_element_type=F32)


def _dot_tn(a, b):
    return lax.dot_general(a, b, (((0,), (0,)), ((), ())), preferred_element_type=F32)


def _cparams(sem):
    return pltpu.CompilerParams(dimension_semantics=sem, vmem_limit_bytes=VMEM_LIMIT)


def _mod_kernel(c_ref, w_ref, b_ref, o_ref):
    a = c_ref[...]
    a = _silu(a)
    o_ref[0] = jnp.dot(a, w_ref[0], preferred_element_type=F32) + b_ref[0]


def _modulation(cc, w_mod, b_mod):
    depth, d, d3 = w_mod.shape
    rows = cc.shape[0]
    tn = 1024
    return pl.pallas_call(
        _mod_kernel,
        out_shape=jax.ShapeDtypeStruct((depth, rows, d3), F32),
        grid=(depth, d3 // tn),
        in_specs=[pl.BlockSpec((rows, d), lambda l, j: (0, 0)),
                  pl.BlockSpec((1, d, tn), lambda l, j: (l, 0, j)),
                  pl.BlockSpec((1, 1, tn), lambda l, j: (l, 0, j))],
        out_specs=pl.BlockSpec((1, rows, tn), lambda l, j: (l, 0, j)),
        compiler_params=_cparams(("arbitrary", "arbitrary")),
        name="modulation",
    )(cc, w_mod, b_mod.reshape(depth, 1, d3))


def _modulate(x, mod_ref, tok0, lc):
    tm = x.shape[0]
    tok = tok0 + lax.broadcasted_iota(jnp.int32, (tm, 1), 0)
    is_ctx = tok < lc
    shift = jnp.where(is_ctx, mod_ref[0, 1, 0:1, :], mod_ref[0, 0, 0:1, :])
    scale = jnp.where(is_ctx, mod_ref[0, 1, 1:2, :], mod_ref[0, 0, 1:2, :])
    return x * (1.0 + scale) + shift


def _shift_vreg_rows(rot, ctx_first, n_ctx_v, down):
    nv = rot.shape[0]
    vpr = GRID_W // 8
    zero = jnp.zeros_like(rot[0:1])
    pieces = []
    for g0 in range(0, nv, vpr):
        g1 = g0 + vpr
        if down:
            edge = jnp.where(ctx_first, rot[g0 - 1:g0], 0.0) if 0 < g0 < n_ctx_v else zero
            pieces += [edge, rot[g0:g1 - 1]]
        else:
            edge = jnp.where(ctx_first, rot[g1:g1 + 1], 0.0) if g1 < n_ctx_v else zero
            pieces += [rot[g0 + 1:g1], edge]
    return jnp.concatenate(pieces, axis=0)


def _conv_rows(acc, cw, cb, ctx_first, n_ctx_v):
    n, tn = acc.shape
    a3 = acc.reshape(n // 8, 8, tn)
    sub = lax.broadcasted_iota(jnp.int32, a3.shape, 1)
    pad = B_CONV // 2
    out = a3 * cw[pad:pad + 1, :] + cb
    for d in range(1, pad + 1):
        rot = pltpu.roll(a3, d, axis=1)
        prev = _shift_vreg_rows(rot, ctx_first, n_ctx_v, True)
        out = out + jnp.where(sub >= d, rot, prev) * cw[pad - d:pad - d + 1, :]
        rot = pltpu.roll(a3, 8 - d, axis=1)
        nxt = _shift_vreg_rows(rot, ctx_first, n_ctx_v, False)
        out = out + jnp.where(sub < 8 - d, rot, nxt) * cw[pad + d:pad + d + 1, :]
    return out.reshape(n, tn)


INPROJ_TN = 256


def _inproj_kernel(x_ref, mod_ref, wm_ref, wg_ref, wdt_ref, cw_ref, cb_ref, bias_ref, aneg_ref, tril_ref, triu_ref,
                   u_ref, dt_ref, cum_ref, h_ref, *, lc):
    t = pl.program_id(1)
    tm = x_ref.shape[1]
    h_ref[...] = _modulate(x_ref[0], mod_ref, t * tm, lc).astype(BF16)
    plain = [j for j in range(0, N_U, INPROJ_TN) if not COL_XBC <= j < COL_A]
    conv = list(range(COL_XBC, COL_A, INPROJ_TN))
    order = []
    while plain or conv:
        if conv:
            order.append(conv.pop(0))
        order += plain[:2]
        plain = plain[2:]
    for j0 in order:
        j1 = j0 + INPROJ_TN
        if j0 < COL_Z:
            w = wg_ref[:, j0 - COL_GATE:j1 - COL_GATE]
        elif j0 < COL_A:
            w = wm_ref[:, j0 - COL_Z + W_COL_Z:j1 - COL_Z + W_COL_Z]
        else:
            w = wm_ref[:, j0 - COL_A + W_COL_A:j1 - COL_A + W_COL_A]
        acc = jnp.dot(h_ref[...], w, preferred_element_type=F32)
        if COL_XBC <= j0 < COL_A:
            acc = _conv_rows(acc, cw_ref[:, j0 - COL_XBC:j1 - COL_XBC], cb_ref[:, j0 - COL_XBC:j1 - COL_XBC],
                             t == 0, lc // 8)
            acc = _silu(acc)
        u_ref[0, :, j0:j1] = acc.astype(u_ref.dtype)

    dt = _softplus(jnp.dot(h_ref[...], wdt_ref[...], preferred_element_type=F32) + bias_ref[...])
    dt_ref[0] = dt
    da = dt * aneg_ref[...]
    fwd_lane = (lax.broadcasted_iota(jnp.int32, (CHUNK, LANES), 1) & B_HEADS) == 0
    for r0 in range(0, tm, CHUNK):
        blk = da[r0:r0 + CHUNK]
        cum_ref[0, r0:r0 + CHUNK, :] = jnp.where(fwd_lane, _cumsum_rows(tril_ref[...], blk),
                                                 _cumsum_rows(triu_ref[...], blk))


def _pick_tile(total, candidates):
    for c in candidates:
        if total % c == 0:
            return c
    raise ValueError(f"no tile for {total}")


def _inproj(xcat, mod, w_all, layer, w_gate, w_dt, conv_w, conv_b, dt_bias, a_log, lc):
    bsz, t_tot, d = xcat.shape
    tm = _pick_tile(t_tot, (768, 512, 256))
    assert lc <= tm and all(c % INPROJ_TN == 0 for c in (COL_Z, COL_XBC, COL_A, N_U, W_COL_Z, W_COL_DT))
    rep = LANES // (2 * B_HEADS)
    bias_row = jnp.tile(dt_bias.reshape(1, 2 * B_HEADS), (1, rep))
    aneg_row = -jnp.exp(jnp.tile(a_log.reshape(1, 2 * B_HEADS), (1, rep)).astype(F32))
    tril = np.tril(np.ones((CHUNK, CHUNK), np.float32))
    full = lambda shape: pl.BlockSpec(shape, lambda b, t: (0,) * len(shape))
    resident = lambda shape: pl.BlockSpec(shape, lambda b, t: (0,) * len(shape), pipeline_mode=pl.Buffered(1))
    seq = lambda w: pl.BlockSpec((1, tm, w), lambda b, t: (b, t, 0))
    return pl.pallas_call(
        functools.partial(_inproj_kernel, lc=lc),
        out_shape=(jax.ShapeDtypeStruct((bsz, t_tot, N_U), BF16),
                   jax.ShapeDtypeStruct((bsz, t_tot, LANES), F32), jax.ShapeDtypeStruct((bsz, t_tot, LANES), F32)),
        grid=(bsz, t_tot // tm),
        in_specs=[seq(d), pl.BlockSpec((1, 2, 3, d), lambda b, t: (b, 0, 0, 0)),
                  pl.BlockSpec((None, d, W_COL_DT), lambda b, t: (layer, 0, 0), pipeline_mode=pl.Buffered(1)),
                  resident((d, 2 * d)), resident((d, LANES)),
                  full((B_CONV, B_CONV_CH)), full((1, B_CONV_CH)),
                  full((1, LANES)), full((1, LANES)), full((CHUNK, CHUNK)), full((CHUNK, CHUNK))],
        out_specs=(seq(N_U), seq(LANES), seq(LANES)),
        scratch_shapes=[pltpu.VMEM((tm, d), BF16)],
        compiler_params=_cparams(("parallel", "parallel")),
        name="inproj",
    )(xcat, mod, w_all, w_gate, w_dt, conv_w, conv_b, bias_row, aneg_row,
      jnp.asarray(tril, BF16), jnp.asarray(tril.T, BF16))


def _level_table():
    t = np.arange(CHUNK)[:, None]
    s = np.arange(CHUNK)[None, :]
    x = t ^ s
    lvl = np.where(x > 0, 2 ** np.floor(np.log2(np.maximum(x, 1))).astype(np.int64), 0)
    return np.where(s <= t, lvl, -1).astype(np.int32)


def _boundary_rows(b, m, backward):
    n = b.shape[0]
    off = m if backward else m - 1
    if m >= 8:
        parts = [jnp.broadcast_to(b[g0 + off:g0 + off + 1, :], (2 * m, b.shape[1])) for g0 in range(0, n, 2 * m)]
        return parts[0] if len(parts) == 1 else jnp.concatenate(parts, axis=0)
    b3 = b.reshape(n // 8, 8, b.shape[1])
    sub = lax.broadcasted_iota(jnp.int32, b3.shape, 1)
    out = None
    for g0 in range(0, 8, 2 * m):
        src = jnp.broadcast_to(b3[:, g0 + off:g0 + off + 1, :], b3.shape)
        out = src if out is None else jnp.where(sub >= g0, src, out)
    return out.reshape(b.shape)


A_HEADS_PER_STEP = 4
A_HEADS_PER_STAGE = 2


def _hgrn2_kernel(q_ref, ff_ref, fb_ref, i_ref, lbl_ref, trif_ref, trib_ref, lvlf_ref, lvlb_ref,
                  of_ref, ob_ref, st_ref, *, layer, n_ctx_chunks):
    n_chunks = q_ref.shape[1] // CHUNK
    hps = q_ref.shape[2] // LANES

    def lower_bound(d):
        lg = lbl_ref[d]
        e = jnp.exp(lg - jnp.max(lg, axis=0, keepdims=True))
        sm = e / jnp.sum(e, axis=0, keepdims=True)
        lb = jnp.zeros((1, lg.shape[1]), F32)
        for l2 in range(1, layer + 1):
            lb = lb + sm[l2:l2 + 1, :]
        return lb

    def rows(c):
        return pl.ds(pl.multiple_of(c * CHUNK, CHUNK), CHUNK)

    lbs = (lower_bound(0), lower_bound(1))
    st_ref[...] = jnp.zeros_like(st_ref)

    z_refs, tri_refs, lvl_refs, acc_refs = (ff_ref, fb_ref), (trif_ref, trib_ref), (lvlf_ref, lvlb_ref), (of_ref, ob_ref)
    lanes = [slice(hh * LANES, (hh + 1) * LANES) for hh in range(hps)]

    def body(i, carry):
        c_b = jnp.where(i < n_ctx_chunks, n_ctx_chunks - 1 - i, n_chunks - 1 - (i - n_ctx_chunks))
        rws = (rows(i), rows(c_b))
        for h0 in range(0, hps, A_HEADS_PER_STAGE):
            heads = list(range(h0, min(h0 + A_HEADS_PER_STAGE, hps)))
            gl = slice(heads[0] * LANES, (heads[-1] + 1) * LANES)
            loc = {hh: slice((hh - h0) * LANES, (hh - h0 + 1) * LANES) for hh in heads}
            row = lax.broadcasted_iota(jnp.int32, (CHUNK, len(heads) * LANES), 0)
            q, k, f, b, vb, lvl = [], [], [], [], [], []
            for d in range(2):
                lb = lbs[d][:, gl]
                fr = lb + (1.0 - lb) * _sigmoid(z_refs[d][0, rws[d], gl].astype(F32))
                q.append(q_ref[0, rws[d], gl].astype(F32))
                k.append(1.0 - fr)
                f.append(jnp.maximum(fr, F_FLOOR))
                b.append(_cumsum_rows(tri_refs[d][...], jnp.log(f[d]) * LOG2E))
                vb.append(i_ref[0, rws[d], gl])
                lvl.append(lvl_refs[d][...])

            chains = [(hh, d) for hh in heads for d in range(2)]
            scores = {}
            for hh, d in chains:
                ls = loc[hh]
                scores[hh, d] = jnp.where(lvl[d] == 0,
                                          _dot_nt(q[d][:, ls].astype(BF16), k[d][:, ls].astype(BF16)), 0.0)
            m = 1
            while m < CHUNK:
                w = []
                for d in range(2):
                    bit = (row & m) != 0
                    qside = jnp.logical_not(bit) if d == 1 else bit
                    if m == 1:
                        e = jnp.where(qside, f[d], 1.0)
                    else:
                        bm = _boundary_rows(b[d], m, d == 1)
                        e = jnp.exp2(jnp.where(qside, b[d] - bm, bm - b[d]))
                    w.append((jnp.where(qside, q[d], k[d]) * e).astype(BF16))
                for hh, d in chains:
                    wh = w[d][:, loc[hh]]
                    scores[hh, d] = jnp.where(lvl[d] == m, _dot_nt(wh, wh), scores[hh, d])
                m *= 2

            qe, kdec, eb = [], [], []
            for d in range(2):
                b_end = b[d][0:1, :] if d == 1 else b[d][CHUNK - 1:CHUNK, :]
                qe.append((q[d] * jnp.exp2(b[d])).astype(BF16))
                kdec.append((k[d] * jnp.exp2(b_end - b[d])).astype(BF16))
                eb.append(jnp.exp2(b_end))
            for hh, d in chains:
                ls = loc[hh]
                st = st_ref[2 * hh + d]
                o = jnp.dot(scores[hh, d].astype(BF16), vb[d][:, ls], preferred_element_type=F32)
                o = o + _dot_nt(qe[d][:, ls], st.astype(BF16))
                st_ref[2 * hh + d] = st * eb[d][:, ls] + _dot_tn(vb[d][:, ls], kdec[d][:, ls])
                acc_refs[d][0, rws[d], lanes[hh]] = o.astype(acc_refs[d].dtype)
        return carry

    lax.fori_loop(0, n_chunks, body, 0)


def _hgrn2(u, lb_logits, layer, lc):
    bsz, t_tot, _ = u.shape
    depth = lb_logits.shape[1]
    hps = A_HEADS_PER_STEP
    wblk = hps * LANES
    tri_f = np.tril(np.ones((CHUNK, CHUNK), np.float32))
    lvl_f = _level_table()
    ca = COL_A // wblk
    seq = lambda off: pl.BlockSpec((1, t_tot, wblk), lambda b, h: (b, 0, ca + off * (A_HEADS // hps) + h))
    const = lambda: pl.BlockSpec((CHUNK, CHUNK), lambda b, h: (0, 0))
    kern = functools.partial(_hgrn2_kernel, layer=layer, n_ctx_chunks=lc // CHUNK)
    out = jax.ShapeDtypeStruct((bsz, t_tot, A_KW), BF16)
    out_spec = pl.BlockSpec((1, t_tot, wblk), lambda b, h: (b, 0, h))
    return pl.pallas_call(
        kern,
        out_shape=(out, out),
        grid=(bsz, A_HEADS // hps),
        in_specs=[seq(0), seq(1), seq(2), seq(3),
                  pl.BlockSpec((2, depth, wblk), lambda b, h: (0, 0, h)),
                  const(), const(), const(), const()],
        out_specs=(out_spec, out_spec),
        scratch_shapes=[pltpu.VMEM((2 * hps, A_DK, A_DK), F32)],
        compiler_params=_cparams(("parallel", "parallel")),
        name="hgrn2",
    )(u, u, u, u, lb_logits,
      jnp.asarray(tri_f, BF16), jnp.asarray(tri_f.T, BF16), jnp.asarray(lvl_f), jnp.asarray(lvl_f.T))


def _ssd_tables(gps):
    hps = gps * B_HPG
    e3 = np.zeros((2, B_HEADS // hps, LANES, hps * LANES), np.float32)
    e2 = np.zeros((2, B_HEADS // hps, LANES, hps * B_HEADDIM), np.float32)
    for d in range(2):
        for s in range(B_HEADS // hps):
            for jj in range(hps):
                lane = 16 * d + hps * s + jj
                for piece in range(3):
                    e3[d, s, 32 * piece + lane, jj * LANES:(jj + 1) * LANES] = 1.0
                for piece in range(2):
                    e2[d, s, 32 * piece + lane, jj * B_HEADDIM:(jj + 1) * B_HEADDIM] = 1.0
    return e3, e2


B_GROUPS_PER_STEP = 2
MASKED_EXPONENT = -1e30


def _ssd_kernel(xs_ref, bm_ref, cm_ref, dt_ref, cum_ref, e3_ref, e2_ref,
                yf_ref, yb_ref, st_ref, rowt_ref, *, n_ctx_chunks):
    gw = B_HPG * B_HEADDIM
    gps = xs_ref.shape[2] // gw
    g0 = pl.program_id(1) * gps
    n_chunks = xs_ref.shape[1] // CHUNK
    c = CHUNK

    def rows(k):
        return pl.ds(pl.multiple_of(k * CHUNK, CHUNK), CHUNK)

    st_ref[...] = jnp.zeros_like(st_ref)
    accs = (yf_ref, yb_ref)

    def body(i, carry):
        c_b = jnp.where(i < n_ctx_chunks, n_ctx_chunks - 1 - i, n_chunks - 1 - (i - n_ctx_chunks))
        rws = (rows(i), rows(c_b))
        lane = lax.broadcasted_iota(jnp.int32, (c, LANES), 1)
        ti = lax.broadcasted_iota(jnp.int32, (c, c), 0)
        si = lax.broadcasted_iota(jnp.int32, (c, c), 1)
        causal = (si <= ti, si >= ti)
        half = lane < B_HEADDIM

        colb, decay_in, gain = [], [], []
        for d in range(2):
            dt = dt_ref[0, rws[d], :]
            cum = cum_ref[0, rws[d], :]
            hi, mid, lo = _split3(cum)
            cum3 = jnp.where(lane < 32, hi, jnp.where(lane < 64, mid, lo))
            colb.append(jnp.dot(cum3, e3_ref[d, 0], preferred_element_type=F32))
            rowt_ref[d] = (jnp.log(dt) - cum).T

            def expand(vals):
                vhi = vals.astype(BF16)
                vlo = (vals - vhi.astype(F32)).astype(BF16)
                return jnp.dot(jnp.where(lane < 32, vhi, vlo), e2_ref[d, 0], preferred_element_type=F32)

            cum_end = cum[0:1, :] if d == 1 else cum[c - 1:c, :]
            decay_in.append(expand(jnp.exp(cum)))
            gain.append(expand(jnp.exp(jnp.minimum(cum_end - cum, 0.0)) * dt))

        chains = [(gg, d) for gg in range(gps) for d in range(2)]
        bmb, cmb, xsf, cbs = {}, {}, {}, {}
        for gg, d in chains:
            bmb[gg, d] = bm_ref[0, rws[d], gg * LANES:(gg + 1) * LANES]
            cmb[gg, d] = cm_ref[0, rws[d], gg * LANES:(gg + 1) * LANES]
            xsf[gg, d] = xs_ref[0, rws[d], gg * gw:(gg + 1) * gw]
            cbs[gg, d] = _dot_nt(cmb[gg, d], bmb[gg, d])
        y_intra = {}
        for gg, d in chains:
            ys = []
            for pair in range(B_HPG // 2):
                xp = xsf[gg, d][:, pair * LANES:(pair + 1) * LANES]
                xblk = jnp.concatenate([jnp.where(half, xp, jnp.zeros_like(xp)),
                                        jnp.where(half, jnp.zeros_like(xp), xp)], axis=0)
                mms = []
                for sub in range(2):
                    j = 2 * pair + sub
                    ln = 16 * d + B_HPG * (g0 + gg) + j
                    col = (gg * B_HPG + j) * LANES
                    arg = colb[d][:, col:col + LANES] + rowt_ref[d, pl.ds(ln, 1), :]
                    lmat = jnp.exp(jnp.where(causal[d], arg, MASKED_EXPONENT))
                    mms.append((cbs[gg, d] * lmat).astype(BF16))
                ys.append(jnp.dot(jnp.concatenate(mms, axis=1), xblk, preferred_element_type=F32))
            y_intra[gg, d] = jnp.concatenate(ys, axis=1)
        for gg, d in chains:
            k = 2 * gg + d
            cs = slice(gg * gw, (gg + 1) * gw)
            state = st_ref[k]
            dec = decay_in[d][:, cs]
            y_inter = jnp.dot(cmb[gg, d], state.astype(BF16), preferred_element_type=F32) * dec
            upd = _dot_tn(bmb[gg, d], xsf[gg, d] * gain[d][:, cs].astype(BF16))
            dec_row = dec[0:1, :] if d == 1 else dec[c - 1:c, :]
            st_ref[k] = state * dec_row + upd
            accs[d][0, rws[d], cs] = (y_intra[gg, d] + y_inter).astype(accs[d].dtype)
        return carry

    lax.fori_loop(0, n_chunks, body, 0)


def _ssd(u, dt, cum, lc):
    bsz, t_tot, _ = u.shape
    gw = B_HPG * B_HEADDIM
    gps = B_GROUPS_PER_STEP
    e3, e2 = _ssd_tables(gps)
    wblk, nblk = gps * gw, gps * LANES
    cx = COL_XBC // wblk
    cbm = (COL_XBC + B_WIDTH) // nblk
    ccm = cbm + B_GROUPS // gps
    kern = functools.partial(_ssd_kernel, n_ctx_chunks=lc // CHUNK)
    out = jax.ShapeDtypeStruct((bsz, t_tot, B_WIDTH), BF16)
    out_spec = pl.BlockSpec((1, t_tot, wblk), lambda b, g: (b, 0, g))
    return pl.pallas_call(
        kern,
        out_shape=(out, out),
        grid=(bsz, B_GROUPS // gps),
        in_specs=[pl.BlockSpec((1, t_tot, wblk), lambda b, g: (b, 0, cx + g)),
                  pl.BlockSpec((1, t_tot, nblk), lambda b, g: (b, 0, cbm + g)),
                  pl.BlockSpec((1, t_tot, nblk), lambda b, g: (b, 0, ccm + g)),
                  pl.BlockSpec((1, t_tot, LANES), lambda b, g: (b, 0, 0)),
                  pl.BlockSpec((1, t_tot, LANES), lambda b, g: (b, 0, 0)),
                  pl.BlockSpec((2, 1, LANES, gps * B_HPG * LANES), lambda b, g: (0, g, 0, 0)),
                  pl.BlockSpec((2, 1, LANES, wblk), lambda b, g: (0, g, 0, 0))],
        out_specs=(out_spec, out_spec),
        scratch_shapes=[pltpu.VMEM((2 * gps, B_STATE, gw), F32), pltpu.VMEM((2, LANES, CHUNK), F32)],
        compiler_params=_cparams(("parallel", "parallel")),
        name="ssd",
    )(u, u, u, dt, cum, jnp.asarray(e3, BF16), jnp.asarray(e2, BF16))


def _chunk_rows(k):
    return pl.ds(pl.multiple_of(k * CHUNK, CHUNK), CHUNK)


def _hgrn2_stages(q_ref, z_refs, i_ref, tri_refs, lvl_refs, acc_refs, st_ref, lbs, rws):
    hps = q_ref.shape[2] // LANES
    lanes = [slice(hh * LANES, (hh + 1) * LANES) for hh in range(hps)]
    chains = [(hh, d) for hh in range(hps) for d in range(2)]
    s = dict(q=[], k=[], f=[], b=[], vb=[], lvl=[], scores={})

    def prep():
        for d in range(2):
            fr = lbs[d] + (1.0 - lbs[d]) * _sigmoid(z_refs[d][0, rws[d], :].astype(F32))
            s["q"].append(q_ref[0, rws[d], :].astype(F32))
            s["k"].append(1.0 - fr)
            s["f"].append(jnp.maximum(fr, F_FLOOR))
            s["b"].append(_cumsum_rows(tri_refs[d][...], jnp.log(s["f"][d]) * LOG2E, pieces=2))
            s["vb"].append(i_ref[0, rws[d], :])
            s["lvl"].append(lvl_refs[d][...])

    def diagonal():
        for hh, d in chains:
            ls = lanes[hh]
            p = _dot_nt(s["q"][d][:, ls].astype(BF16), s["k"][d][:, ls].astype(BF16))
            s["scores"][hh, d] = jnp.where(s["lvl"][d] == 0, p, 0.0)

    def level_rows(m):
        w, qrows = [], []
        for d in range(2):
            b, q, k = s["b"][d], s["q"][d], s["k"][d]
            parts, qpart = [], []
            for g0 in range(0, CHUNK, 2 * m):
                lo, hi = slice(g0, g0 + m), slice(g0 + m, g0 + 2 * m)
                if d == 0:
                    bnd = b[g0 + m - 1:g0 + m, :]
                    parts += [k[lo] * jnp.exp2(bnd - b[lo]), q[hi] * jnp.exp2(b[hi] - bnd)]
                    qpart.append(1)
                else:
                    bnd = b[g0 + m:g0 + m + 1, :]
                    parts += [q[lo] * jnp.exp2(b[lo] - bnd), k[hi] * jnp.exp2(bnd - b[hi])]
                    qpart.append(0)
            w.append(jnp.concatenate(parts, axis=0).astype(BF16))
            qsel = [parts[2 * g + qpart[g]] for g in range(len(qpart))]
            qrows.append((qsel[0] if len(qsel) == 1 else jnp.concatenate(qsel, axis=0)).astype(BF16))
        for hh, d in chains:
            ls = lanes[hh]
            p = _dot_nt(qrows[d][:, ls], w[d][:, ls])
            sc, lv, out = s["scores"][hh, d], s["lvl"][d], []
            for g, g0 in enumerate(range(0, CHUNK, 2 * m)):
                lo, hi = slice(g0, g0 + m), slice(g0 + m, g0 + 2 * m)
                qs = hi if d == 0 else lo
                upd = jnp.where(lv[qs] == m, p[g * m:(g + 1) * m], sc[qs])
                out += [sc[lo], upd] if d == 0 else [upd, sc[hi]]
            s["scores"][hh, d] = jnp.concatenate(out, axis=0)

    def level(m):
        if m >= 8:
            return level_rows(m)
        row = lax.broadcasted_iota(jnp.int32, (CHUNK, hps * LANES), 0)
        w = []
        for d in range(2):
            bit = (row & m) != 0
            qside = jnp.logical_not(bit) if d == 1 else bit
            if m == 1:
                e = jnp.where(qside, s["f"][d], 1.0)
            else:
                bm = _boundary_rows(s["b"][d], m, d == 1)
                e = jnp.exp2(jnp.where(qside, s["b"][d] - bm, bm - s["b"][d]))
            w.append((jnp.where(qside, s["q"][d], s["k"][d]) * e).astype(BF16))
        for hh, d in chains:
            wh = w[d][:, lanes[hh]]
            s["scores"][hh, d] = jnp.where(s["lvl"][d] == m, _dot_nt(wh, wh), s["scores"][hh, d])

    def final():
        qe, kdec, eb = [], [], []
        for d in range(2):
            b = s["b"][d]
            b_end = b[0:1, :] if d == 1 else b[CHUNK - 1:CHUNK, :]
            qe.append((s["q"][d] * jnp.exp2(b)).astype(BF16))
            kdec.append((s["k"][d] * jnp.exp2(b_end - b)).astype(BF16))
            eb.append(jnp.exp2(b_end))
        for hh, d in chains:
            ls = lanes[hh]
            st = st_ref[2 * hh + d]
            o = jnp.dot(s["scores"][hh, d].astype(BF16), s["vb"][d][:, ls], preferred_element_type=F32)
            o = o + _dot_nt(qe[d][:, ls], st.astype(BF16))
            st_ref[2 * hh + d] = st * eb[d][:, ls] + _dot_tn(s["vb"][d][:, ls], kdec[d][:, ls])
            acc_refs[d][0, rws[d], ls] = o.astype(acc_refs[d].dtype)

    levels = []
    m = 1
    while m < CHUNK:
        levels.append(functools.partial(level, m))
        m *= 2
    return [prep, diagonal] + levels + [final]


def _ssd_stages(xs_ref, bm_ref, cm_ref, dt_ref, cum_ref, e3_ref, e2_ref, acc_refs, st_ref, rowt_ref, g0, rws):
    gw = B_HPG * B_HEADDIM
    gps = xs_ref.shape[2] // gw
    c = CHUNK
    chains = [(gg, d) for gg in range(gps) for d in range(2)]
    s = dict(colb=[], decay_in=[], gain=[], bmb={}, cmb={}, xsf={}, cbs={}, y_intra={})

    def prep():
        lane = lax.broadcasted_iota(jnp.int32, (c, LANES), 1)
        for d in range(2):
            dt = dt_ref[0, rws[d], :]
            cum = cum_ref[0, rws[d], :]
            hi, mid, lo = _split3(cum)
            cum3 = jnp.where(lane < 32, hi, jnp.where(lane < 64, mid, lo))
            s["colb"].append(jnp.dot(cum3, e3_ref[d, 0], preferred_element_type=F32))
            rowt_ref[d] = (jnp.log(dt) - cum).T

            def expand(vals):
                vhi = vals.astype(BF16)
                vlo = (vals - vhi.astype(F32)).astype(BF16)
                return jnp.dot(jnp.where(lane < 32, vhi, vlo), e2_ref[d, 0], preferred_element_type=F32)

            cum_end = cum[0:1, :] if d == 1 else cum[c - 1:c, :]
            s["decay_in"].append(expand(jnp.exp(cum)))
            s["gain"].append(expand(jnp.exp(jnp.minimum(cum_end - cum, 0.0)) * dt))

    def cb():
        for gg, d in chains:
            s["bmb"][gg, d] = bm_ref[0, rws[d], gg * LANES:(gg + 1) * LANES]
            s["cmb"][gg, d] = cm_ref[0, rws[d], gg * LANES:(gg + 1) * LANES]
            s["xsf"][gg, d] = xs_ref[0, rws[d], gg * gw:(gg + 1) * gw]
            s["cbs"][gg, d] = _dot_nt(s["cmb"][gg, d], s["bmb"][gg, d])

    def intra(gg, d):
        ti = lax.broadcasted_iota(jnp.int32, (c, c), 0)
        si = lax.broadcasted_iota(jnp.int32, (c, c), 1)
        causal = (si >= ti) if d == 1 else (si <= ti)
        half = lax.broadcasted_iota(jnp.int32, (c, LANES), 1) < B_HEADDIM
        ys = []
        for pair in range(B_HPG // 2):
            xp = s["xsf"][gg, d][:, pair * LANES:(pair + 1) * LANES]
            xblk = jnp.concatenate([jnp.where(half, xp, jnp.zeros_like(xp)),
                                    jnp.where(half, jnp.zeros_like(xp), xp)], axis=0)
            mms = []
            for sub in range(2):
                j = 2 * pair + sub
                ln = 16 * d + B_HPG * (g0 + gg) + j
                col = (gg * B_HPG + j) * LANES
                arg = s["colb"][d][:, col:col + LANES] + rowt_ref[d, pl.ds(ln, 1), :]
                lmat = jnp.exp(jnp.where(causal, arg, MASKED_EXPONENT))
                mms.append((s["cbs"][gg, d] * lmat).astype(BF16))
            ys.append(jnp.dot(jnp.concatenate(mms, axis=1), xblk, preferred_element_type=F32))
        s["y_intra"][gg, d] = jnp.concatenate(ys, axis=1)

    def state(gg, d):
        k = 2 * gg + d
        cs = slice(gg * gw, (gg + 1) * gw)
        st = st_ref[k]
        dec = s["decay_in"][d][:, cs]
        y_inter = jnp.dot(s["cmb"][gg, d], st.astype(BF16), preferred_element_type=F32) * dec
        upd = _dot_tn(s["bmb"][gg, d], s["xsf"][gg, d] * s["gain"][d][:, cs].astype(BF16))
        dec_row = dec[0:1, :] if d == 1 else dec[c - 1:c, :]
        st_ref[k] = st * dec_row + upd
        acc_refs[d][0, rws[d], cs] = (s["y_intra"][gg, d] + y_inter).astype(acc_refs[d].dtype)

    return ([prep, cb] + [functools.partial(intra, gg, d) for gg, d in chains]
            + [functools.partial(state, gg, d) for gg, d in chains])


def _interleave(a, b):
    out, ib = [], 0
    for ia, fa in enumerate(a):
        out.append(fa)
        want = (ia + 1) * len(b) // len(a)
        while ib < want:
            out.append(b[ib])
            ib += 1
    return out + b[ib:]


def _scan_kernel(q_ref, ff_ref, fb_ref, i_ref, lbl_ref, trif_ref, trib_ref, lvlf_ref, lvlb_ref,
                 xs_ref, bm_ref, cm_ref, dt_ref, cum_ref, e3_ref, e2_ref,
                 of_ref, ob_ref, yf_ref, yb_ref, sta_ref, stb_ref, rowt_ref, *, layer, n_ctx_chunks):
    n_chunks = q_ref.shape[1] // CHUNK
    gps = xs_ref.shape[2] // (B_HPG * B_HEADDIM)
    g0 = pl.program_id(1) * gps

    def lower_bound(d):
        lg = lbl_ref[d]
        e = jnp.exp(lg - jnp.max(lg, axis=0, keepdims=True))
        sm = e / jnp.sum(e, axis=0, keepdims=True)
        lb = jnp.zeros((1, lg.shape[1]), F32)
        for l2 in range(1, layer + 1):
            lb = lb + sm[l2:l2 + 1, :]
        return lb

    lbs = (lower_bound(0), lower_bound(1))
    sta_ref[...] = jnp.zeros_like(sta_ref)
    stb_ref[...] = jnp.zeros_like(stb_ref)

    def body(i, carry):
        c_b = jnp.where(i < n_ctx_chunks, n_ctx_chunks - 1 - i, n_chunks - 1 - (i - n_ctx_chunks))
        rws = (_chunk_rows(i), _chunk_rows(c_b))
        h = _hgrn2_stages(q_ref, (ff_ref, fb_ref), i_ref, (trif_ref, trib_ref), (lvlf_ref, lvlb_ref),
                          (of_ref, ob_ref), sta_ref, lbs, rws)
        s = _ssd_stages(xs_ref, bm_ref, cm_ref, dt_ref, cum_ref, e3_ref, e2_ref, (yf_ref, yb_ref), stb_ref,
                        rowt_ref, g0, rws)
        for stage in _interleave(h, s):
            stage()
        return carry

    lax.fori_loop(0, n_chunks, body, 0)


SCAN_SPLIT = 2


def _scan(u, dt, cum, lb_logits, layer, lc):
    bsz, t_tot, _ = u.shape
    depth = lb_logits.shape[1]
    hps, gps = A_HEADS // SCAN_SPLIT, B_GROUPS // SCAN_SPLIT
    gw = B_HPG * B_HEADDIM
    awid, wblk, nblk = hps * LANES, gps * gw, gps * LANES
    tri_f = np.tril(np.ones((CHUNK, CHUNK), np.float32))
    lvl_f = _level_table()
    e3, e2 = _ssd_tables(gps)
    ca, cx = COL_A // awid, COL_XBC // wblk
    cbm = (COL_XBC + B_WIDTH) // nblk
    ccm = cbm + B_GROUPS // gps
    aseq = lambda off: pl.BlockSpec((1, t_tot, awid), lambda b, s: (b, 0, ca + off * SCAN_SPLIT + s))
    const = lambda: pl.BlockSpec((CHUNK, CHUNK), lambda b, s: (0, 0))
    lane_seq = lambda: pl.BlockSpec((1, t_tot, LANES), lambda b, s: (b, 0, 0))
    a_out = jax.ShapeDtypeStruct((bsz, t_tot, A_KW), BF16)
    b_out = jax.ShapeDtypeStruct((bsz, t_tot, B_WIDTH), BF16)
    a_spec = pl.BlockSpec((1, t_tot, awid), lambda b, s: (b, 0, s))
    b_spec = pl.BlockSpec((1, t_tot, wblk), lambda b, s: (b, 0, s))
    kern = functools.partial(_scan_kernel, layer=layer, n_ctx_chunks=lc // CHUNK)
    of, ob, yf, yb = pl.pallas_call(
        kern,
        out_shape=(a_out, a_out, b_out, b_out),
        grid=(bsz, SCAN_SPLIT),
        in_specs=[aseq(0), aseq(1), aseq(2), aseq(3),
                  pl.BlockSpec((2, depth, awid), lambda b, s: (0, 0, s)),
                  const(), const(), const(), const(),
                  pl.BlockSpec((1, t_tot, wblk), lambda b, s: (b, 0, cx + s)),
                  pl.BlockSpec((1, t_tot, nblk), lambda b, s: (b, 0, cbm + s)),
                  pl.BlockSpec((1, t_tot, nblk), lambda b, s: (b, 0, ccm + s)),
                  lane_seq(), lane_seq(),
                  pl.BlockSpec((2, 1, LANES, gps * B_HPG * LANES), lambda b, s: (0, s, 0, 0)),
                  pl.BlockSpec((2, 1, LANES, wblk), lambda b, s: (0, s, 0, 0))],
        out_specs=(a_spec, a_spec, b_spec, b_spec),
        scratch_shapes=[pltpu.VMEM((2 * hps, A_DK, A_DK), F32), pltpu.VMEM((2 * gps, B_STATE, gw), F32),
                        pltpu.VMEM((2, LANES, CHUNK), F32)],
        compiler_params=_cparams(("parallel", "parallel")),
        name="scan",
    )(u, u, u, u, lb_logits,
      jnp.asarray(tri_f, BF16), jnp.asarray(tri_f.T, BF16), jnp.asarray(lvl_f), jnp.asarray(lvl_f.T),
      u, u, u, dt, cum, jnp.asarray(e3, BF16), jnp.asarray(e2, BF16))
    return (of, ob), (yf, yb)


def _group_rms(v, width):
    parts = []
    for c0 in range(0, v.shape[1], width):
        p = v[:, c0:c0 + width]
        parts.append(p * lax.rsqrt(jnp.mean(p * p, axis=-1, keepdims=True) + RMS_EPS))
    return jnp.concatenate(parts, axis=1)


MERGE_SUB = 256


def _merge_kernel(x_ref, mod_ref, of_ref, ob_ref, g_ref, yf_ref, yb_ref, xs_ref, z_ref, ga_ref, gb_ref,
                  anw_ref, dsk_ref, bnw_ref, wpa_ref, wpb_ref, wo_ref, lng_ref, lnb_ref, o_ref, *, lc, t_off, alpha):
    t = pl.program_id(1)
    tm = x_ref.shape[1]
    sub = min(tm, MERGE_SUB)
    n_sub = tm // sub
    rs = [slice(k * sub, (k + 1) * sub) for k in range(n_sub)]
    ybf, pab, mrg, out = {}, {}, {}, {}

    def norms(k):
        r = rs[k]
        oa = of_ref[0, r, :].astype(F32) + ob_ref[0, r, :].astype(F32)
        ya = _group_rms(oa, A_DK) * anw_ref[...] * _silu(g_ref[0, r, :].astype(F32))
        yb = yf_ref[0, r, :].astype(F32) + yb_ref[0, r, :].astype(F32) + dsk_ref[...] * xs_ref[0, r, :].astype(F32)
        yb = _group_rms(yb * _silu(z_ref[0, r, :].astype(F32)), B_WIDTH // B_GROUPS) * bnw_ref[...]
        ybf[k] = (ya.astype(BF16), yb.astype(BF16))

    def project(k):
        ya, yb = ybf.pop(k)
        pab[k] = (jnp.dot(ya, wpa_ref[...], preferred_element_type=F32),
                  jnp.dot(yb, wpb_ref[...], preferred_element_type=F32))

    def gates(k):
        r = rs[k]
        pa, pb = pab.pop(k)
        mrg[k] = _sigmoid(ga_ref[0, r, :]) * pa.astype(BF16) + _sigmoid(gb_ref[0, r, :]) * pb.astype(BF16)

    def outproj(k):
        out[k] = jnp.dot(mrg.pop(k), wo_ref[...], preferred_element_type=F32)

    def residual_norm(k):
        r = rs[k]
        tok = (t + t_off) * tm + k * sub + lax.broadcasted_iota(jnp.int32, (sub, 1), 0)
        gate = jnp.where(tok < lc, mod_ref[0, 1, 2:3, :], mod_ref[0, 0, 2:3, :])
        y = alpha * x_ref[0, r, :] + gate * out.pop(k)
        mu = jnp.mean(y, axis=-1, keepdims=True)
        yc = y - mu
        var = jnp.mean(yc * yc, axis=-1, keepdims=True)
        o_ref[0, r, :] = yc * lax.rsqrt(var + LN_EPS) * lng_ref[...] + lnb_ref[...]

    stages = (norms, project, gates, outproj, residual_norm)
    for step in range(n_sub + len(stages) - 1):
        for s, stage in reversed(list(enumerate(stages))):
            if 0 <= step - s < n_sub:
                stage(step - s)


def _merge(xcat, mod, oa, yb, u, a_norm_w, d_skip, b_norm_w, wpa, wpb, wo, ln_g, ln_b, lc, alpha, latent_only):
    bsz, t_tot, d = xcat.shape
    tm = _pick_tile(lc, (256,)) if latent_only else _pick_tile(t_tot, (768, 512, 256))
    t_off = lc // tm if latent_only else 0
    n_t = t_tot // tm - t_off
    seq = lambda w, col=0: pl.BlockSpec((1, tm, w), lambda b, t: (b, t + t_off, col // w))
    full = lambda a: pl.BlockSpec(a.shape, lambda b, t: (0,) * a.ndim, pipeline_mode=pl.Buffered(1))
    kern = functools.partial(_merge_kernel, lc=lc, t_off=t_off, alpha=alpha)
    lng = ln_g.reshape(1, d)
    lnb = ln_b.reshape(1, d)
    anw = jnp.tile(a_norm_w.reshape(1, A_DK), (1, A_HEADS))
    dsk = jnp.repeat(d_skip, B_HEADDIM).reshape(1, B_WIDTH)
    bnw = b_norm_w.reshape(1, B_WIDTH)
    return pl.pallas_call(
        kern,
        out_shape=jax.ShapeDtypeStruct((bsz, n_t * tm, d), F32),
        grid=(bsz, n_t),
        in_specs=[seq(d), pl.BlockSpec((1, 2, 3, d), lambda b, t: (b, 0, 0, 0)),
                  seq(A_KW), seq(A_KW), seq(A_KW, COL_A + 4 * A_KW),
                  seq(B_WIDTH), seq(B_WIDTH), seq(B_WIDTH, COL_XBC), seq(B_WIDTH, COL_Z),
                  seq(d, COL_GATE), seq(d, COL_GATE + d),
                  full(anw), full(dsk), full(bnw), full(wpa), full(wpb), full(wo), full(lng), full(lnb)],
        out_specs=pl.BlockSpec((1, tm, d), lambda b, t: (b, t, 0)),
        compiler_params=_cparams(("parallel", "parallel")),
        name="merge",
    )(xcat, mod, oa[0], oa[1], u, yb[0], yb[1], u, u, u, u, anw, dsk, bnw, wpa, wpb, wo, lng, lnb)


def kernel(x, c, ctx, c_ctx, w_mod, b_mod, w_in, a_lb_logits, a_norm_w, b_conv_w, b_conv_b, b_dt_bias, b_a_log,
           b_d, b_norm_w, w_proj_a, w_proj_b, w_out, ln_g, ln_b):
    bsz, seq, d = x.shape
    lc = ctx.shape[1]
    depth = w_mod.shape[0]
    assert 2 * d == COL_Z - COL_GATE and d == 2 * A_KW and d == B_WIDTH and seq % GRID_W == 0
    assert lc % 256 == 0 and (lc + seq) % 256 == 0 and seq % CHUNK == 0
    assert w_in.shape[2] == W_COL_GATE + 2 * d
    alpha = (2 * depth) ** 0.25

    rows = -(-(bsz + 1) // 8) * 8
    cc = jnp.zeros((rows, d), F32).at[:bsz].set(c).at[bsz].set(c_ctx)
    mod_all = _modulation(cc, w_mod, b_mod)
    ml = mod_all[:, :bsz].reshape(depth, bsz, 1, 3, d)
    mc = jnp.broadcast_to(mod_all[:, bsz].reshape(depth, 1, 1, 3, d), (depth, bsz, 1, 3, d))
    mod_all = jnp.concatenate([ml, mc], axis=2)

    w_all = w_in.astype(BF16)
    w_gate = w_all[:, :, W_COL_GATE:]
    w_dt = jnp.tile(w_all[:, :, W_COL_DT:W_COL_GATE], (1, 1, LANES // (2 * B_HEADS)))
    wpa = w_proj_a.astype(BF16)
    wpb = w_proj_b.astype(BF16)
    wo = w_out.astype(BF16)

    xcat = jnp.concatenate([ctx, x], axis=1)
    for l in range(depth):
        mod = mod_all[l]
        u, dt, cum = _inproj(xcat, mod, w_all, l, w_gate[l], w_dt[l], b_conv_w[l], b_conv_b[l].reshape(1, -1),
                             b_dt_bias[l], b_a_log[l], lc)
        oa, yb = _scan(u, dt, cum, a_lb_logits, l, lc)
        xcat = _merge(xcat, mod, oa, yb, u, a_norm_w[l], b_d[l], b_norm_w[l], wpa[l], wpb[l], wo[l],
                      ln_g[l], ln_b[l], lc, alpha, latent_only=(l == depth - 1))
    return xcat
```

```python
import functools

import numpy as np
import jax
import jax.numpy as jnp
from jax import lax
from jax.experimental import pallas as pl
from jax.experimental.pallas import tpu as pltpu

F32 = jnp.float32
BF16 = jnp.bfloat16

A_HEADS = 4
A_DK = 128
A_KW = A_HEADS * A_DK
B_WIDTH = 1024
B_HEADDIM = 64
B_HEADS = B_WIDTH // B_HEADDIM
B_GROUPS = 4
B_HPG = B_HEADS // B_GROUPS
B_STATE = 128
B_CONV = 5
GRID_W = 64
LN_EPS = 1e-5
RMS_EPS = 1e-6
F_FLOOR = 1e-30
LOG2E = 1.4426950408889634

CHUNK = 128
LANES = 128
VMEM_LIMIT = 56 * 1024 * 1024

B_CONV_CH = B_WIDTH + 2 * B_GROUPS * B_STATE
COL_GATE = 0
COL_Z = 2048
COL_XBC = COL_Z + B_WIDTH
COL_A = COL_XBC + B_CONV_CH
N_U = COL_A + 5 * A_KW
W_COL_A = 0
W_COL_Z = 5 * A_KW
W_COL_DT = W_COL_Z + B_WIDTH + B_CONV_CH
W_COL_GATE = W_COL_DT + 2 * B_HEADS


def _sigmoid(x):
    return 0.5 * jnp.tanh(0.5 * x) + 0.5


def _silu(x):
    h = 0.5 * x
    return h + h * jnp.tanh(h)


def _softplus(x):
    return jnp.maximum(x, 0.0) + jnp.log(1.0 + jnp.exp(-jnp.abs(x)))


def _split3(x):
    hi = x.astype(BF16)
    r = x - hi.astype(F32)
    mid = r.astype(BF16)
    lo = (r - mid.astype(F32)).astype(BF16)
    return hi, mid, lo


def _cumsum_rows(tri, x, pieces=3):
    d = lambda a: jnp.dot(tri, a, preferred_element_type=F32)
    if pieces == 2:
        hi = x.astype(BF16)
        return d(hi) + d((x - hi.astype(F32)).astype(BF16))
    hi, mid, lo = _split3(x)
    return d(hi) + d(mid) + d(lo)


def _dot_nt(a, b):
    return lax.dot_general(a, b, (((1,), (1,)), ((), ())), preferred_element_type=F32)


def _dot_tn(a, b):
    return lax.dot_general(a, b, (((0,), (0,)), ((), ())), preferred_element_type=F32)


def _cparams(sem):
    return pltpu.CompilerParams(dimension_semantics=sem, vmem_limit_bytes=VMEM_LIMIT)


def _pick_tile(total, candidates):
    for c in candidates:
        if total % c == 0:
            return c
    raise ValueError(f"no tile for {total}")


def _x_specs(xparts, tm, lc, t_off=0):
    d = xparts[0].shape[2]
    if len(xparts) == 1:
        return [pl.BlockSpec((1, tm, d), lambda b, t: (b, t + t_off, 0))]
    assert tm % lc == 0 and t_off == 0
    nb = tm // lc
    xspec = lambda k: pl.BlockSpec((1, lc, d), lambda b, t: (b, jnp.maximum(t * nb + k - 1, 0), 0))
    return [pl.BlockSpec((1, lc, d), lambda b, t: (b, 0, 0))] + [xspec(k) for k in range(nb)]


def _x_operands(xparts, tm, lc):
    return list(xparts) if len(xparts) == 1 else [xparts[0]] + [xparts[1]] * (tm // lc)


def _x_rows(x_refs, t, r0, n):
    if len(x_refs) == 1:
        return x_refs[0][0, r0:r0 + n, :]
    k = r0 // n
    blk = x_refs[1 + k][0]
    return jnp.where(t == 0, x_refs[0][0], blk) if k == 0 else blk


def _mod_kernel(c_ref, w_ref, b_ref, o_ref):
    a = c_ref[...]
    a = _silu(a)
    o_ref[0] = jnp.dot(a, w_ref[0], preferred_element_type=F32) + b_ref[0]


def _modulation(cc, w_mod, b_mod):
    depth, d, d3 = w_mod.shape
    rows = cc.shape[0]
    tn = 1024
    return pl.pallas_call(
        _mod_kernel,
        out_shape=jax.ShapeDtypeStruct((depth, rows, d3), F32),
        grid=(depth, d3 // tn),
        in_specs=[pl.BlockSpec((rows, d), lambda l, j: (0, 0)),
                  pl.BlockSpec((1, d, tn), lambda l, j: (l, 0, j)),
                  pl.BlockSpec((1, 1, tn), lambda l, j: (l, 0, j))],
        out_specs=pl.BlockSpec((1, rows, tn), lambda l, j: (l, 0, j)),
        compiler_params=_cparams(("arbitrary", "arbitrary")),
        name="modulation",
    )(cc, w_mod, b_mod.reshape(depth, 1, d3))


def _modulate(x, mod_ref, tok0, lc):
    tm = x.shape[0]
    tok = tok0 + lax.broadcasted_iota(jnp.int32, (tm, 1), 0)
    is_ctx = tok < lc
    shift = jnp.where(is_ctx, mod_ref[0, 1, 0:1, :], mod_ref[0, 0, 0:1, :])
    scale = jnp.where(is_ctx, mod_ref[0, 1, 1:2, :], mod_ref[0, 0, 1:2, :])
    return x * (1.0 + scale) + shift


def _shift_vreg_rows(rot, ctx_first, n_ctx_v, down):
    nv = rot.shape[0]
    vpr = GRID_W // 8
    zero = jnp.zeros_like(rot[0:1])
    pieces = []
    for g0 in range(0, nv, vpr):
        g1 = g0 + vpr
        if down:
            edge = jnp.where(ctx_first, rot[g0 - 1:g0], 0.0) if 0 < g0 < n_ctx_v else zero
            pieces += [edge, rot[g0:g1 - 1]]
        else:
            edge = jnp.where(ctx_first, rot[g1:g1 + 1], 0.0) if g1 < n_ctx_v else zero
            pieces += [rot[g0 + 1:g1], edge]
    return jnp.concatenate(pieces, axis=0)


def _conv_rows(acc, cw, cb, ctx_first, n_ctx_v):
    n, tn = acc.shape
    a3 = acc.reshape(n // 8, 8, tn)
    sub = lax.broadcasted_iota(jnp.int32, a3.shape, 1)
    pad = B_CONV // 2
    out = a3 * cw[pad:pad + 1, :] + cb
    for d in range(1, pad + 1):
        rot = pltpu.roll(a3, d, axis=1)
        prev = _shift_vreg_rows(rot, ctx_first, n_ctx_v, True)
        out = out + jnp.where(sub >= d, rot, prev) * cw[pad - d:pad - d + 1, :]
        rot = pltpu.roll(a3, 8 - d, axis=1)
        nxt = _shift_vreg_rows(rot, ctx_first, n_ctx_v, False)
        out = out + jnp.where(sub < 8 - d, rot, nxt) * cw[pad + d:pad + d + 1, :]
    return out.reshape(n, tn)


INPROJ_TN = 256


def _inproj_kernel(*refs, n_x, lc):
    x_refs = refs[:n_x]
    (mod_ref, wm_ref, wg_ref, wdt_ref, cw_ref, cb_ref, bias_ref, aneg_ref, tril_ref, triu_ref,
     u_ref, dt_ref, cum_ref, h_ref) = refs[n_x:]
    t = pl.program_id(1)
    tm = u_ref.shape[1]
    for r0 in range(0, tm, lc):
        h_ref[r0:r0 + lc, :] = _modulate(_x_rows(x_refs, t, r0, lc), mod_ref, t * tm + r0, lc).astype(BF16)
    plain = [j for j in range(0, N_U, INPROJ_TN) if not COL_XBC <= j < COL_A]
    conv = list(range(COL_XBC, COL_A, INPROJ_TN))
    order = []
    while plain or conv:
        if conv:
            order.append(conv.pop(0))
        order += plain[:2]
        plain = plain[2:]
    for j0 in order:
        j1 = j0 + INPROJ_TN
        if j0 < COL_Z:
            w = wg_ref[:, j0 - COL_GATE:j1 - COL_GATE]
        elif j0 < COL_A:
            w = wm_ref[:, j0 - COL_Z + W_COL_Z:j1 - COL_Z + W_COL_Z]
        else:
            w = wm_ref[:, j0 - COL_A + W_COL_A:j1 - COL_A + W_COL_A]
        acc = jnp.dot(h_ref[...], w, preferred_element_type=F32)
        if COL_XBC <= j0 < COL_A:
            acc = _conv_rows(acc, cw_ref[:, j0 - COL_XBC:j1 - COL_XBC], cb_ref[:, j0 - COL_XBC:j1 - COL_XBC],
                             t == 0, lc // 8)
            acc = _silu(acc)
        u_ref[0, :, j0:j1] = acc.astype(u_ref.dtype)

    dt = _softplus(jnp.dot(h_ref[...], wdt_ref[...], preferred_element_type=F32) + bias_ref[...])
    dt_ref[0] = dt
    da = dt * aneg_ref[...]
    fwd_lane = (lax.broadcasted_iota(jnp.int32, (CHUNK, LANES), 1) & B_HEADS) == 0
    for r0 in range(0, tm, CHUNK):
        blk = da[r0:r0 + CHUNK]
        cum_ref[0, r0:r0 + CHUNK, :] = jnp.where(fwd_lane, _cumsum_rows(tril_ref[...], blk),
                                                 _cumsum_rows(triu_ref[...], blk))


def _inproj(xparts, mod, w_all, layer, w_gate, w_dt, conv_w, conv_b, dt_bias, a_log, lc):
    bsz, d = xparts[0].shape[0], xparts[0].shape[2]
    t_tot = sum(p.shape[1] for p in xparts)
    tm = _pick_tile(t_tot, (768, 512, 256))
    assert tm % lc == 0 and all(c % INPROJ_TN == 0 for c in (COL_Z, COL_XBC, COL_A, N_U, W_COL_Z, W_COL_DT))
    rep = LANES // (2 * B_HEADS)
    bias_row = jnp.tile(dt_bias.reshape(1, 2 * B_HEADS), (1, rep))
    aneg_row = -jnp.exp(jnp.tile(a_log.reshape(1, 2 * B_HEADS), (1, rep)).astype(F32))
    tril = np.tril(np.ones((CHUNK, CHUNK), np.float32))
    full = lambda shape: pl.BlockSpec(shape, lambda b, t: (0,) * len(shape))
    resident = lambda shape: pl.BlockSpec(shape, lambda b, t: (0,) * len(shape), pipeline_mode=pl.Buffered(1))
    seq = lambda w: pl.BlockSpec((1, tm, w), lambda b, t: (b, t, 0))
    x_specs = _x_specs(xparts, tm, lc)
    return pl.pallas_call(
        functools.partial(_inproj_kernel, n_x=len(x_specs), lc=lc),
        out_shape=(jax.ShapeDtypeStruct((bsz, t_tot, N_U), BF16),
                   jax.ShapeDtypeStruct((bsz, t_tot, LANES), F32), jax.ShapeDtypeStruct((bsz, t_tot, LANES), F32)),
        grid=(bsz, t_tot // tm),
        in_specs=x_specs + [pl.BlockSpec((1, 2, 3, d), lambda b, t: (b, 0, 0, 0)),
                            pl.BlockSpec((None, d, W_COL_DT), lambda b, t: (layer, 0, 0),
                                         pipeline_mode=pl.Buffered(1)),
                            resident((d, 2 * d)), resident((d, LANES)),
                            full((B_CONV, B_CONV_CH)), full((1, B_CONV_CH)),
                            full((1, LANES)), full((1, LANES)), full((CHUNK, CHUNK)), full((CHUNK, CHUNK))],
        out_specs=(seq(N_U), seq(LANES), seq(LANES)),
        scratch_shapes=[pltpu.VMEM((tm, d), BF16)],
        compiler_params=_cparams(("parallel", "parallel")),
        name="inproj",
    )(*_x_operands(xparts, tm, lc), mod, w_all, w_gate, w_dt, conv_w, conv_b, bias_row, aneg_row,
      jnp.asarray(tril, BF16), jnp.asarray(tril.T, BF16))


def _level_table():
    t = np.arange(CHUNK)[:, None]
    s = np.arange(CHUNK)[None, :]
    x = t ^ s
    lvl = np.where(x > 0, 2 ** np.floor(np.log2(np.maximum(x, 1))).astype(np.int64), 0)
    return np.where(s <= t, lvl, -1).astype(np.int32)


def _boundary_rows(b, m, backward):
    n = b.shape[0]
    off = m if backward else m - 1
    b3 = b.reshape(n // 8, 8, b.shape[1])
    sub = lax.broadcasted_iota(jnp.int32, b3.shape, 1)
    out = None
    for g0 in range(0, 8, 2 * m):
        src = jnp.broadcast_to(b3[:, g0 + off:g0 + off + 1, :], b3.shape)
        out = src if out is None else jnp.where(sub >= g0, src, out)
    return out.reshape(b.shape)


def _chunk_rows(k):
    return pl.ds(pl.multiple_of(k * CHUNK, CHUNK), CHUNK)


def _hgrn2_stages(q_ref, z_refs, i_ref, tri_refs, lvl_refs, acc_refs, st_ref, lbs, rws):
    hps = q_ref.shape[2] // LANES
    lanes = [slice(hh * LANES, (hh + 1) * LANES) for hh in range(hps)]
    chains = [(hh, d) for hh in range(hps) for d in range(2)]
    s = dict(q=[], k=[], f=[], b=[], vb=[], lvl=[], scores={})

    def prep():
        for d in range(2):
            fr = lbs[d] + (1.0 - lbs[d]) * _sigmoid(z_refs[d][0, rws[d], :].astype(F32))
            s["q"].append(q_ref[0, rws[d], :].astype(F32))
            s["k"].append(1.0 - fr)
            s["f"].append(jnp.maximum(fr, F_FLOOR))
            s["b"].append(_cumsum_rows(tri_refs[d][...], jnp.log(s["f"][d]) * LOG2E, pieces=2))
            s["vb"].append(i_ref[0, rws[d], :])
            s["lvl"].append(lvl_refs[d][...])

    def diagonal():
        for hh, d in chains:
            ls = lanes[hh]
            p = _dot_nt(s["q"][d][:, ls].astype(BF16), s["k"][d][:, ls].astype(BF16))
            s["scores"][hh, d] = jnp.where(s["lvl"][d] == 0, p, 0.0)

    def level_rows(m):
        w, qrows = [], []
        for d in range(2):
            b, q, k = s["b"][d], s["q"][d], s["k"][d]
            parts, qpart = [], []
            for g0 in range(0, CHUNK, 2 * m):
                lo, hi = slice(g0, g0 + m), slice(g0 + m, g0 + 2 * m)
                if d == 0:
                    bnd = b[g0 + m - 1:g0 + m, :]
                    parts += [k[lo] * jnp.exp2(bnd - b[lo]), q[hi] * jnp.exp2(b[hi] - bnd)]
                    qpart.append(1)
                else:
                    bnd = b[g0 + m:g0 + m + 1, :]
                    parts += [q[lo] * jnp.exp2(b[lo] - bnd), k[hi] * jnp.exp2(bnd - b[hi])]
                    qpart.append(0)
            w.append(jnp.concatenate(parts, axis=0).astype(BF16))
            qsel = [parts[2 * g + qpart[g]] for g in range(len(qpart))]
            qrows.append((qsel[0] if len(qsel) == 1 else jnp.concatenate(qsel, axis=0)).astype(BF16))
        for hh, d in chains:
            ls = lanes[hh]
            p = _dot_nt(qrows[d][:, ls], w[d][:, ls])
            sc, lv, out = s["scores"][hh, d], s["lvl"][d], []
            for g, g0 in enumerate(range(0, CHUNK, 2 * m)):
                lo, hi = slice(g0, g0 + m), slice(g0 + m, g0 + 2 * m)
                qs = hi if d == 0 else lo
                upd = jnp.where(lv[qs] == m, p[g * m:(g + 1) * m], sc[qs])
                out += [sc[lo], upd] if d == 0 else [upd, sc[hi]]
            s["scores"][hh, d] = jnp.concatenate(out, axis=0)

    def level(m):
        if m >= 8:
            return level_rows(m)
        row = lax.broadcasted_iota(jnp.int32, (CHUNK, hps * LANES), 0)
        w = []
        for d in range(2):
            bit = (row & m) != 0
            qside = jnp.logical_not(bit) if d == 1 else bit
            if m == 1:
                e = jnp.where(qside, s["f"][d], 1.0)
            else:
                bm = _boundary_rows(s["b"][d], m, d == 1)
                e = jnp.exp2(jnp.where(qside, s["b"][d] - bm, bm - s["b"][d]))
            w.append((jnp.where(qside, s["q"][d], s["k"][d]) * e).astype(BF16))
        for hh, d in chains:
            wh = w[d][:, lanes[hh]]
            s["scores"][hh, d] = jnp.where(s["lvl"][d] == m, _dot_nt(wh, wh), s["scores"][hh, d])

    def final():
        qe, kdec, eb = [], [], []
        for d in range(2):
            b = s["b"][d]
            b_end = b[0:1, :] if d == 1 else b[CHUNK - 1:CHUNK, :]
            qe.append((s["q"][d] * jnp.exp2(b)).astype(BF16))
            kdec.append((s["k"][d] * jnp.exp2(b_end - b)).astype(BF16))
            eb.append(jnp.exp2(b_end))
        for hh, d in chains:
            ls = lanes[hh]
            st = st_ref[2 * hh + d]
            o = jnp.dot(s["scores"][hh, d].astype(BF16), s["vb"][d][:, ls], preferred_element_type=F32)
            o = o + _dot_nt(qe[d][:, ls], st.astype(BF16))
            st_ref[2 * hh + d] = st * eb[d][:, ls] + _dot_tn(s["vb"][d][:, ls], kdec[d][:, ls])
            acc_refs[d][0, rws[d], ls] = o.astype(acc_refs[d].dtype)

    levels = []
    m = 1
    while m < CHUNK:
        levels.append(functools.partial(level, m))
        m *= 2
    return [prep, diagonal] + levels + [final]


MASKED_EXPONENT = -1e30


def _ssd_tables(gps):
    hps = gps * B_HPG
    e3 = np.zeros((2, B_HEADS // hps, LANES, hps * LANES), np.float32)
    e2 = np.zeros((2, B_HEADS // hps, LANES, hps * B_HEADDIM), np.float32)
    for d in range(2):
        for s in range(B_HEADS // hps):
            for jj in range(hps):
                lane = 16 * d + hps * s + jj
                for piece in range(3):
                    e3[d, s, 32 * piece + lane, jj * LANES:(jj + 1) * LANES] = 1.0
                for piece in range(2):
                    e2[d, s, 32 * piece + lane, jj * B_HEADDIM:(jj + 1) * B_HEADDIM] = 1.0
    return e3, e2


def _ssd_stages(xs_ref, bm_ref, cm_ref, dt_ref, cum_ref, e3_ref, e2_ref, acc_refs, st_ref, rowt_ref, g0, rws):
    gw = B_HPG * B_HEADDIM
    gps = xs_ref.shape[2] // gw
    c = CHUNK
    chains = [(gg, d) for gg in range(gps) for d in range(2)]
    s = dict(colb=[], decay_in=[], gain=[], bmb={}, cmb={}, xsf={}, cbs={}, y_intra={})

    def prep():
        lane = lax.broadcasted_iota(jnp.int32, (c, LANES), 1)
        for d in range(2):
            dt = dt_ref[0, rws[d], :]
            cum = cum_ref[0, rws[d], :]
            hi, mid, lo = _split3(cum)
            cum3 = jnp.where(lane < 32, hi, jnp.where(lane < 64, mid, lo))
            s["colb"].append(jnp.dot(cum3, e3_ref[d, 0], preferred_element_type=F32))
            rowt_ref[d] = (jnp.log(dt) - cum).T

            def expand(vals):
                vhi = vals.astype(BF16)
                vlo = (vals - vhi.astype(F32)).astype(BF16)
                return jnp.dot(jnp.where(lane < 32, vhi, vlo), e2_ref[d, 0], preferred_element_type=F32)

            cum_end = cum[0:1, :] if d == 1 else cum[c - 1:c, :]
            s["decay_in"].append(expand(jnp.exp(cum)))
            s["gain"].append(expand(jnp.exp(jnp.minimum(cum_end - cum, 0.0)) * dt))

    def cb():
        for gg, d in chains:
            s["bmb"][gg, d] = bm_ref[0, rws[d], gg * LANES:(gg + 1) * LANES]
            s["cmb"][gg, d] = cm_ref[0, rws[d], gg * LANES:(gg + 1) * LANES]
            s["xsf"][gg, d] = xs_ref[0, rws[d], gg * gw:(gg + 1) * gw]
            s["cbs"][gg, d] = _dot_nt(s["cmb"][gg, d], s["bmb"][gg, d])

    def intra(gg, d):
        ti = lax.broadcasted_iota(jnp.int32, (c, c), 0)
        si = lax.broadcasted_iota(jnp.int32, (c, c), 1)
        causal = (si >= ti) if d == 1 else (si <= ti)
        half = lax.broadcasted_iota(jnp.int32, (c, LANES), 1) < B_HEADDIM
        ys = []
        for pair in range(B_HPG // 2):
            xp = s["xsf"][gg, d][:, pair * LANES:(pair + 1) * LANES]
            xblk = jnp.concatenate([jnp.where(half, xp, jnp.zeros_like(xp)),
                                    jnp.where(half, jnp.zeros_like(xp), xp)], axis=0)
            mms = []
            for sub in range(2):
                j = 2 * pair + sub
                ln = 16 * d + B_HPG * (g0 + gg) + j
                col = (gg * B_HPG + j) * LANES
                arg = s["colb"][d][:, col:col + LANES] + rowt_ref[d, pl.ds(ln, 1), :]
                lmat = jnp.exp(jnp.where(causal, arg, MASKED_EXPONENT))
                mms.append((s["cbs"][gg, d] * lmat).astype(BF16))
            ys.append(jnp.dot(jnp.concatenate(mms, axis=1), xblk, preferred_element_type=F32))
        s["y_intra"][gg, d] = jnp.concatenate(ys, axis=1)

    def state(gg, d):
        k = 2 * gg + d
        cs = slice(gg * gw, (gg + 1) * gw)
        st = st_ref[k]
        dec = s["decay_in"][d][:, cs]
        y_inter = jnp.dot(s["cmb"][gg, d], st.astype(BF16), preferred_element_type=F32) * dec
        upd = _dot_tn(s["bmb"][gg, d], s["xsf"][gg, d] * s["gain"][d][:, cs].astype(BF16))
        dec_row = dec[0:1, :] if d == 1 else dec[c - 1:c, :]
        st_ref[k] = st * dec_row + upd
        acc_refs[d][0, rws[d], cs] = (s["y_intra"][gg, d] + y_inter).astype(acc_refs[d].dtype)

    return ([prep, cb] + [functools.partial(intra, gg, d) for gg, d in chains]
            + [functools.partial(state, gg, d) for gg, d in chains])


def _interleave(a, b):
    out, ib = [], 0
    for ia, fa in enumerate(a):
        out.append(fa)
        want = (ia + 1) * len(b) // len(a)
        while ib < want:
            out.append(b[ib])
            ib += 1
    return out + b[ib:]


def _scan_kernel(q_ref, ff_ref, fb_ref, i_ref, lbl_ref, trif_ref, trib_ref, lvlf_ref, lvlb_ref,
                 xs_ref, bm_ref, cm_ref, dt_ref, cum_ref, e3_ref, e2_ref,
                 of_ref, ob_ref, yf_ref, yb_ref, sta_ref, stb_ref, rowt_ref, *, layer, n_ctx_chunks):
    n_chunks = q_ref.shape[1] // CHUNK
    gps = xs_ref.shape[2] // (B_HPG * B_HEADDIM)
    g0 = pl.program_id(1) * gps

    def lower_bound(d):
        lg = lbl_ref[d]
        e = jnp.exp(lg - jnp.max(lg, axis=0, keepdims=True))
        sm = e / jnp.sum(e, axis=0, keepdims=True)
        lb = jnp.zeros((1, lg.shape[1]), F32)
        for l2 in range(1, layer + 1):
            lb = lb + sm[l2:l2 + 1, :]
        return lb

    lbs = (lower_bound(0), lower_bound(1))
    sta_ref[...] = jnp.zeros_like(sta_ref)
    stb_ref[...] = jnp.zeros_like(stb_ref)

    def body(i, carry):
        c_b = jnp.where(i < n_ctx_chunks, n_ctx_chunks - 1 - i, n_chunks - 1 - (i - n_ctx_chunks))
        rws = (_chunk_rows(i), _chunk_rows(c_b))
        h = _hgrn2_stages(q_ref, (ff_ref, fb_ref), i_ref, (trif_ref, trib_ref), (lvlf_ref, lvlb_ref),
                          (of_ref, ob_ref), sta_ref, lbs, rws)
        s = _ssd_stages(xs_ref, bm_ref, cm_ref, dt_ref, cum_ref, e3_ref, e2_ref, (yf_ref, yb_ref), stb_ref,
                        rowt_ref, g0, rws)
        for stage in _interleave(h, s):
            stage()
        return carry

    lax.fori_loop(0, n_chunks, body, 0)


SCAN_SPLIT = 2


def _scan(u, dt, cum, lb_logits, layer, lc):
    bsz, t_tot, _ = u.shape
    depth = lb_logits.shape[1]
    hps, gps = A_HEADS // SCAN_SPLIT, B_GROUPS // SCAN_SPLIT
    gw = B_HPG * B_HEADDIM
    awid, wblk, nblk = hps * LANES, gps * gw, gps * LANES
    tri_f = np.tril(np.ones((CHUNK, CHUNK), np.float32))
    lvl_f = _level_table()
    e3, e2 = _ssd_tables(gps)
    ca, cx = COL_A // awid, COL_XBC // wblk
    cbm = (COL_XBC + B_WIDTH) // nblk
    ccm = cbm + B_GROUPS // gps
    aseq = lambda off: pl.BlockSpec((1, t_tot, awid), lambda b, s: (b, 0, ca + off * SCAN_SPLIT + s))
    const = lambda: pl.BlockSpec((CHUNK, CHUNK), lambda b, s: (0, 0))
    lane_seq = lambda: pl.BlockSpec((1, t_tot, LANES), lambda b, s: (b, 0, 0))
    a_out = jax.ShapeDtypeStruct((bsz, t_tot, A_KW), BF16)
    b_out = jax.ShapeDtypeStruct((bsz, t_tot, B_WIDTH), BF16)
    a_spec = pl.BlockSpec((1, t_tot, awid), lambda b, s: (b, 0, s))
    b_spec = pl.BlockSpec((1, t_tot, wblk), lambda b, s: (b, 0, s))
    kern = functools.partial(_scan_kernel, layer=layer, n_ctx_chunks=lc // CHUNK)
    of, ob, yf, yb = pl.pallas_call(
        kern,
        out_shape=(a_out, a_out, b_out, b_out),
        grid=(bsz, SCAN_SPLIT),
        in_specs=[aseq(0), aseq(1), aseq(2), aseq(3),
                  pl.BlockSpec((2, depth, awid), lambda b, s: (0, 0, s)),
                  const(), const(), const(), const(),
                  pl.BlockSpec((1, t_tot, wblk), lambda b, s: (b, 0, cx + s)),
                  pl.BlockSpec((1, t_tot, nblk), lambda b, s: (b, 0, cbm + s)),
                  pl.BlockSpec((1, t_tot, nblk), lambda b, s: (b, 0, ccm + s)),
                  lane_seq(), lane_seq(),
                  pl.BlockSpec((2, 1, LANES, gps * B_HPG * LANES), lambda b, s: (0, s, 0, 0)),
                  pl.BlockSpec((2, 1, LANES, wblk), lambda b, s: (0, s, 0, 0))],
        out_specs=(a_spec, a_spec, b_spec, b_spec),
        scratch_shapes=[pltpu.VMEM((2 * hps, A_DK, A_DK), F32), pltpu.VMEM((2 * gps, B_STATE, gw), F32),
                        pltpu.VMEM((2, LANES, CHUNK), F32)],
        compiler_params=_cparams(("parallel", "parallel")),
        name="scan",
    )(u, u, u, u, lb_logits,
      jnp.asarray(tri_f, BF16), jnp.asarray(tri_f.T, BF16), jnp.asarray(lvl_f), jnp.asarray(lvl_f.T),
      u, u, u, dt, cum, jnp.asarray(e3, BF16), jnp.asarray(e2, BF16))
    return (of, ob), (yf, yb)


def _group_rms(v, width):
    parts = []
    for c0 in range(0, v.shape[1], width):
        p = v[:, c0:c0 + width]
        parts.append(p * lax.rsqrt(jnp.mean(p * p, axis=-1, keepdims=True) + RMS_EPS))
    return jnp.concatenate(parts, axis=1)


def _merge_kernel(*refs, n_x, lc, t_off, alpha):
    x_refs = refs[:n_x]
    (mod_ref, of_ref, ob_ref, g_ref, yf_ref, yb_ref, xs_ref, z_ref, ga_ref, gb_ref,
     anw_ref, dsk_ref, bnw_ref, wpa_ref, wpb_ref, wo_ref, lng_ref, lnb_ref, o_ref) = refs[n_x:]
    t = pl.program_id(1)
    tm = o_ref.shape[1]
    sub = min(tm, lc)
    n_sub = tm // sub
    rs = [slice(k * sub, (k + 1) * sub) for k in range(n_sub)]
    ybf, pab, mrg, out = {}, {}, {}, {}

    def norms(k):
        r = rs[k]
        oa = of_ref[0, r, :].astype(F32) + ob_ref[0, r, :].astype(F32)
        ya = _group_rms(oa, A_DK) * anw_ref[...] * _silu(g_ref[0, r, :].astype(F32))
        yb = yf_ref[0, r, :].astype(F32) + yb_ref[0, r, :].astype(F32) + dsk_ref[...] * xs_ref[0, r, :].astype(F32)
        yb = _group_rms(yb * _silu(z_ref[0, r, :].astype(F32)), B_WIDTH // B_GROUPS) * bnw_ref[...]
        ybf[k] = (ya.astype(BF16), yb.astype(BF16))

    def project(k):
        ya, yb = ybf.pop(k)
        pab[k] = (jnp.dot(ya, wpa_ref[...], preferred_element_type=F32),
                  jnp.dot(yb, wpb_ref[...], preferred_element_type=F32))

    def gates(k):
        r = rs[k]
        pa, pb = pab.pop(k)
        mrg[k] = _sigmoid(ga_ref[0, r, :]) * pa.astype(BF16) + _sigmoid(gb_ref[0, r, :]) * pb.astype(BF16)

    def outproj(k):
        out[k] = jnp.dot(mrg.pop(k), wo_ref[...], preferred_element_type=F32)

    def residual_norm(k):
        r = rs[k]
        tok = (t + t_off) * tm + k * sub + lax.broadcasted_iota(jnp.int32, (sub, 1), 0)
        gate = jnp.where(tok < lc, mod_ref[0, 1, 2:3, :], mod_ref[0, 0, 2:3, :])
        y = alpha * _x_rows(x_refs, t, k * sub, sub) + gate * out.pop(k)
        mu = jnp.mean(y, axis=-1, keepdims=True)
        yc = y - mu
        var = jnp.mean(yc * yc, axis=-1, keepdims=True)
        o_ref[0, r, :] = yc * lax.rsqrt(var + LN_EPS) * lng_ref[...] + lnb_ref[...]

    stages = (norms, project, gates, outproj, residual_norm)
    for step in range(n_sub + len(stages) - 1):
        for s, stage in reversed(list(enumerate(stages))):
            if 0 <= step - s < n_sub:
                stage(step - s)


def _merge(xparts, mod, oa, yb, u, a_norm_w, d_skip, b_norm_w, wpa, wpb, wo, ln_g, ln_b, lc, alpha, latent_only):
    bsz, d = xparts[0].shape[0], xparts[0].shape[2]
    t_tot = sum(p.shape[1] for p in xparts)
    tm = _pick_tile(lc, (256,)) if latent_only else _pick_tile(t_tot, (768, 512, 256))
    t_off = lc // tm if latent_only else 0
    n_t = t_tot // tm - t_off
    seq = lambda w, col=0: pl.BlockSpec((1, tm, w), lambda b, t: (b, t + t_off, col // w))
    full = lambda a: pl.BlockSpec(a.shape, lambda b, t: (0,) * a.ndim, pipeline_mode=pl.Buffered(1))
    x_specs = _x_specs(xparts, tm, lc, t_off)
    kern = functools.partial(_merge_kernel, n_x=len(x_specs), lc=lc, t_off=t_off, alpha=alpha)
    lng = ln_g.reshape(1, d)
    lnb = ln_b.reshape(1, d)
    anw = jnp.tile(a_norm_w.reshape(1, A_DK), (1, A_HEADS))
    dsk = jnp.repeat(d_skip, B_HEADDIM).reshape(1, B_WIDTH)
    bnw = b_norm_w.reshape(1, B_WIDTH)
    return pl.pallas_call(
        kern,
        out_shape=jax.ShapeDtypeStruct((bsz, n_t * tm, d), F32),
        grid=(bsz, n_t),
        in_specs=x_specs + [pl.BlockSpec((1, 2, 3, d), lambda b, t: (b, 0, 0, 0)),
                            seq(A_KW), seq(A_KW), seq(A_KW, COL_A + 4 * A_KW),
                            seq(B_WIDTH), seq(B_WIDTH), seq(B_WIDTH, COL_XBC), seq(B_WIDTH, COL_Z),
                            seq(d, COL_GATE), seq(d, COL_GATE + d),
                            full(anw), full(dsk), full(bnw), full(wpa), full(wpb), full(wo), full(lng), full(lnb)],
        out_specs=pl.BlockSpec((1, tm, d), lambda b, t: (b, t, 0)),
        compiler_params=_cparams(("parallel", "parallel")),
        name="merge",
    )(*_x_operands(xparts, tm, lc), mod, oa[0], oa[1], u, yb[0], yb[1], u, u, u, u,
      anw, dsk, bnw, wpa, wpb, wo, lng, lnb)


def kernel(x, c, ctx, c_ctx, w_mod, b_mod, w_in, a_lb_logits, a_norm_w, b_conv_w, b_conv_b, b_dt_bias, b_a_log,
           b_d, b_norm_w, w_proj_a, w_proj_b, w_out, ln_g, ln_b):
    bsz, seq, d = x.shape
    lc = ctx.shape[1]
    depth = w_mod.shape[0]
    assert 2 * d == COL_Z - COL_GATE and d == 2 * A_KW and d == B_WIDTH and seq % GRID_W == 0
    assert lc % 256 == 0 and (lc + seq) % 256 == 0 and seq % CHUNK == 0
    assert w_in.shape[2] == W_COL_GATE + 2 * d
    alpha = (2 * depth) ** 0.25

    rows = -(-(bsz + 1) // 8) * 8
    cc = jnp.zeros((rows, d), F32).at[:bsz].set(c).at[bsz].set(c_ctx)
    mod_all = _modulation(cc, w_mod, b_mod)
    ml = mod_all[:, :bsz].reshape(depth, bsz, 1, 3, d)
    mc = jnp.broadcast_to(mod_all[:, bsz].reshape(depth, 1, 1, 3, d), (depth, bsz, 1, 3, d))
    mod_all = jnp.concatenate([ml, mc], axis=2)

    w_all = w_in.astype(BF16)
    w_gate = w_all[:, :, W_COL_GATE:]
    w_dt = jnp.tile(w_all[:, :, W_COL_DT:W_COL_GATE], (1, 1, LANES // (2 * B_HEADS)))
    wpa = w_proj_a.astype(BF16)
    wpb = w_proj_b.astype(BF16)
    wo = w_out.astype(BF16)

    xparts = (ctx, x)
    for l in range(depth):
        mod = mod_all[l]
        u, dt, cum = _inproj(xparts, mod, w_all, l, w_gate[l], w_dt[l], b_conv_w[l], b_conv_b[l].reshape(1, -1),
                             b_dt_bias[l], b_a_log[l], lc)
        oa, yb = _scan(u, dt, cum, a_lb_logits, l, lc)
        xparts = (_merge(xparts, mod, oa, yb, u, a_norm_w[l], b_d[l], b_norm_w[l], wpa[l], wpb[l], wo[l],
                         ln_g[l], ln_b[l], lc, alpha, latent_only=(l == depth - 1)),)
    return xparts[0]
```

```python
import functools

import numpy as np
import jax
import jax.numpy as jnp
from jax import lax
from jax.experimental import pallas as pl
from jax.experimental.pallas import tpu as pltpu

F32 = jnp.float32
BF16 = jnp.bfloat16

A_HEADS = 4
A_DK = 128
A_KW = A_HEADS * A_DK
B_WIDTH = 1024
B_HEADDIM = 64
B_HEADS = B_WIDTH // B_HEADDIM
B_GROUPS = 4
B_HPG = B_HEADS // B_GROUPS
B_STATE = 128
B_CONV = 5
GRID_W = 64
LN_EPS = 1e-5
RMS_EPS = 1e-6
F_FLOOR = 1e-30
LOG2E = 1.4426950408889634

CHUNK = 128
LANES = 128
VMEM_LIMIT = 56 * 1024 * 1024

B_CONV_CH = B_WIDTH + 2 * B_GROUPS * B_STATE
COL_GATE = 0
COL_Z = 2048
COL_XBC = COL_Z + B_WIDTH
COL_A = COL_XBC + B_CONV_CH
N_U = COL_A + 5 * A_KW
W_COL_A = 0
W_COL_Z = 5 * A_KW
W_COL_DT = W_COL_Z + B_WIDTH + B_CONV_CH
W_COL_GATE = W_COL_DT + 2 * B_HEADS


def _sigmoid(x):
    return 0.5 * jnp.tanh(0.5 * x) + 0.5


def _silu(x):
    h = 0.5 * x
    return h + h * jnp.tanh(h)


def _softplus(x):
    return jnp.maximum(x, 0.0) + jnp.log(1.0 + jnp.exp(-jnp.abs(x)))


def _split3(x):
    hi = x.astype(BF16)
    r = x - hi.astype(F32)
    mid = r.astype(BF16)
    lo = (r - mid.astype(F32)).astype(BF16)
    return hi, mid, lo


def _cumsum_rows(tri, x, pieces=3):
    d = lambda a: jnp.dot(tri, a, preferred_element_type=F32)
    if pieces == 2:
        hi = x.astype(BF16)
        return d(hi) + d((x - hi.astype(F32)).astype(BF16))
    hi, mid, lo = _split3(x)
    return d(hi) + d(mid) + d(lo)


def _dot_nt(a, b):
    return lax.dot_general(a, b, (((1,), (1,)), ((), ())), preferred_element_type=F32)


def _dot_tn(a, b):
    return lax.dot_general(a, b, (((0,), (0,)), ((), ())), preferred_element_type=F32)


def _cparams(sem):
    return pltpu.CompilerParams(dimension_semantics=sem, vmem_limit_bytes=VMEM_LIMIT)


def _pick_tile(total, candidates):
    for c in candidates:
        if total % c == 0:
            return c
    raise ValueError(f"no tile for {total}")


def _x_specs(xparts, tm, lc, t_off=0):
    d = xparts[0].shape[2]
    if len(xparts) == 1:
        return [pl.BlockSpec((1, tm, d), lambda b, t: (b, t + t_off, 0))]
    assert tm % lc == 0 and t_off == 0
    nb = tm // lc
    xspec = lambda k: pl.BlockSpec((1, lc, d), lambda b, t: (b, jnp.maximum(t * nb + k - 1, 0), 0))
    return [pl.BlockSpec((1, lc, d), lambda b, t: (b, 0, 0))] + [xspec(k) for k in range(nb)]


def _x_operands(xparts, tm, lc):
    return list(xparts) if len(xparts) == 1 else [xparts[0]] + [xparts[1]] * (tm // lc)


def _x_rows(x_refs, t, r0, n):
    if len(x_refs) == 1:
        return x_refs[0][0, r0:r0 + n, :]
    k = r0 // n
    blk = x_refs[1 + k][0]
    return jnp.where(t == 0, x_refs[0][0], blk) if k == 0 else blk


def _mod_kernel(c_ref, w_ref, b_ref, o_ref):
    a = c_ref[...]
    a = _silu(a)
    o_ref[0] = jnp.dot(a, w_ref[0], preferred_element_type=F32) + b_ref[0]


def _modulation(cc, w_mod, b_mod):
    depth, d, d3 = w_mod.shape
    rows = cc.shape[0]
    tn = 1024
    return pl.pallas_call(
        _mod_kernel,
        out_shape=jax.ShapeDtypeStruct((depth, rows, d3), F32),
        grid=(depth, d3 // tn),
        in_specs=[pl.BlockSpec((rows, d), lambda l, j: (0, 0)),
                  pl.BlockSpec((1, d, tn), lambda l, j: (l, 0, j)),
                  pl.BlockSpec((1, 1, tn), lambda l, j: (l, 0, j))],
        out_specs=pl.BlockSpec((1, rows, tn), lambda l, j: (l, 0, j)),
        compiler_params=_cparams(("arbitrary", "arbitrary")),
        name="modulation",
    )(cc, w_mod, b_mod.reshape(depth, 1, d3))


def _modulate(x, mod_ref, tok0, lc):
    tm = x.shape[0]
    tok = tok0 + lax.broadcasted_iota(jnp.int32, (tm, 1), 0)
    is_ctx = tok < lc
    shift = jnp.where(is_ctx, mod_ref[0, 1, 0:1, :], mod_ref[0, 0, 0:1, :])
    scale = jnp.where(is_ctx, mod_ref[0, 1, 1:2, :], mod_ref[0, 0, 1:2, :])
    return x * (1.0 + scale) + shift


def _shift_vreg_rows(rot, ctx_first, n_ctx_v, down):
    nv = rot.shape[0]
    vpr = GRID_W // 8
    zero = jnp.zeros_like(rot[0:1])
    pieces = []
    for g0 in range(0, nv, vpr):
        g1 = g0 + vpr
        if down:
            edge = jnp.where(ctx_first, rot[g0 - 1:g0], 0.0) if 0 < g0 < n_ctx_v else zero
            pieces += [edge, rot[g0:g1 - 1]]
        else:
            edge = jnp.where(ctx_first, rot[g1:g1 + 1], 0.0) if g1 < n_ctx_v else zero
            pieces += [rot[g0 + 1:g1], edge]
    return jnp.concatenate(pieces, axis=0)


def _conv_rows(acc, cw, cb, ctx_first, n_ctx_v):
    n, tn = acc.shape
    a3 = acc.reshape(n // 8, 8, tn)
    sub = lax.broadcasted_iota(jnp.int32, a3.shape, 1)
    pad = B_CONV // 2
    out = a3 * cw[pad:pad + 1, :] + cb
    for d in range(1, pad + 1):
        rot = pltpu.roll(a3, d, axis=1)
        prev = _shift_vreg_rows(rot, ctx_first, n_ctx_v, True)
        out = out + jnp.where(sub >= d, rot, prev) * cw[pad - d:pad - d + 1, :]
        rot = pltpu.roll(a3, 8 - d, axis=1)
        nxt = _shift_vreg_rows(rot, ctx_first, n_ctx_v, False)
        out = out + jnp.where(sub < 8 - d, rot, nxt) * cw[pad + d:pad + d + 1, :]
    return out.reshape(n, tn)


INPROJ_TN = 256


def _inproj_kernel(*refs, n_x, lc):
    x_refs = refs[:n_x]
    (mod_ref, wm_ref, wg_ref, wdt_ref, cw_ref, cb_ref, bias_ref, aneg_ref, tril_ref, triu_ref,
     u_ref, dt_ref, cum_ref, h_ref) = refs[n_x:]
    t = pl.program_id(1)
    tm = u_ref.shape[1]
    for r0 in range(0, tm, lc):
        h_ref[r0:r0 + lc, :] = _modulate(_x_rows(x_refs, t, r0, lc), mod_ref, t * tm + r0, lc).astype(BF16)
    plain = [j for j in range(0, N_U, INPROJ_TN) if not COL_XBC <= j < COL_A]
    conv = list(range(COL_XBC, COL_A, INPROJ_TN))
    order = []
    while plain or conv:
        if conv:
            order.append(conv.pop(0))
        order += plain[:2]
        plain = plain[2:]
    for j0 in order:
        j1 = j0 + INPROJ_TN
        if j0 < COL_Z:
            w = wg_ref[:, j0 - COL_GATE:j1 - COL_GATE]
        elif j0 < COL_A:
            w = wm_ref[:, j0 - COL_Z + W_COL_Z:j1 - COL_Z + W_COL_Z]
        else:
            w = wm_ref[:, j0 - COL_A + W_COL_A:j1 - COL_A + W_COL_A]
        acc = jnp.dot(h_ref[...], w, preferred_element_type=F32)
        if COL_XBC <= j0 < COL_A:
            acc = _conv_rows(acc, cw_ref[:, j0 - COL_XBC:j1 - COL_XBC], cb_ref[:, j0 - COL_XBC:j1 - COL_XBC],
                             t == 0, lc // 8)
            acc = _silu(acc)
        u_ref[0, :, j0:j1] = acc.astype(u_ref.dtype)

    dt = _softplus(jnp.dot(h_ref[...], wdt_ref[...], preferred_element_type=F32) + bias_ref[...])
    dt_ref[0] = dt
    da = dt * aneg_ref[...]
    fwd_lane = (lax.broadcasted_iota(jnp.int32, (CHUNK, LANES), 1) & B_HEADS) == 0
    for r0 in range(0, tm, CHUNK):
        blk = da[r0:r0 + CHUNK]
        cum_ref[0, r0:r0 + CHUNK, :] = jnp.where(fwd_lane, _cumsum_rows(tril_ref[...], blk),
                                                 _cumsum_rows(triu_ref[...], blk))


def _inproj(xparts, mod, w_all, layer, w_gate, w_dt, conv_w, conv_b, dt_bias, a_log, lc):
    bsz, d = xparts[0].shape[0], xparts[0].shape[2]
    t_tot = sum(p.shape[1] for p in xparts)
    tm = _pick_tile(t_tot, (768, 512, 256))
    assert tm % lc == 0 and all(c % INPROJ_TN == 0 for c in (COL_Z, COL_XBC, COL_A, N_U, W_COL_Z, W_COL_DT))
    rep = LANES // (2 * B_HEADS)
    bias_row = jnp.tile(dt_bias.reshape(1, 2 * B_HEADS), (1, rep))
    aneg_row = -jnp.exp(jnp.tile(a_log.reshape(1, 2 * B_HEADS), (1, rep)).astype(F32))
    tril = np.tril(np.ones((CHUNK, CHUNK), np.float32))
    full = lambda shape: pl.BlockSpec(shape, lambda b, t: (0,) * len(shape))
    resident = lambda shape: pl.BlockSpec(shape, lambda b, t: (0,) * len(shape), pipeline_mode=pl.Buffered(1))
    seq = lambda w: pl.BlockSpec((1, tm, w), lambda b, t: (b, t, 0))
    x_specs = _x_specs(xparts, tm, lc)
    return pl.pallas_call(
        functools.partial(_inproj_kernel, n_x=len(x_specs), lc=lc),
        out_shape=(jax.ShapeDtypeStruct((bsz, t_tot, N_U), BF16),
                   jax.ShapeDtypeStruct((bsz, t_tot, LANES), F32), jax.ShapeDtypeStruct((bsz, t_tot, LANES), F32)),
        grid=(bsz, t_tot // tm),
        in_specs=x_specs + [pl.BlockSpec((1, 2, 3, d), lambda b, t: (b, 0, 0, 0)),
                            pl.BlockSpec((None, d, W_COL_DT), lambda b, t: (layer, 0, 0),
                                         pipeline_mode=pl.Buffered(1)),
                            resident((d, 2 * d)), resident((d, LANES)),
                            full((B_CONV, B_CONV_CH)), full((1, B_CONV_CH)),
                            full((1, LANES)), full((1, LANES)), full((CHUNK, CHUNK)), full((CHUNK, CHUNK))],
        out_specs=(seq(N_U), seq(LANES), seq(LANES)),
        scratch_shapes=[pltpu.VMEM((tm, d), BF16)],
        compiler_params=_cparams(("parallel", "parallel")),
        name="inproj",
    )(*_x_operands(xparts, tm, lc), mod, w_all, w_gate, w_dt, conv_w, conv_b, bias_row, aneg_row,
      jnp.asarray(tril, BF16), jnp.asarray(tril.T, BF16))


def _level_table():
    t = np.arange(CHUNK)[:, None]
    s = np.arange(CHUNK)[None, :]
    x = t ^ s
    lvl = np.where(x > 0, 2 ** np.floor(np.log2(np.maximum(x, 1))).astype(np.int64), 0)
    return np.where(s <= t, lvl, -1).astype(np.int32)


def _boundary_rows(b, m, backward):
    n = b.shape[0]
    off = m if backward else m - 1
    b3 = b.reshape(n // 8, 8, b.shape[1])
    sub = lax.broadcasted_iota(jnp.int32, b3.shape, 1)
    out = None
    for g0 in range(0, 8, 2 * m):
        src = jnp.broadcast_to(b3[:, g0 + off:g0 + off + 1, :], b3.shape)
        out = src if out is None else jnp.where(sub >= g0, src, out)
    return out.reshape(b.shape)


def _chunk_rows(k):
    return pl.ds(pl.multiple_of(k * CHUNK, CHUNK), CHUNK)


def _hgrn2_stages(q_ref, z_refs, i_ref, tri_refs, lvl_refs, acc_refs, st_ref, lbs, rws):
    hps = q_ref.shape[2] // LANES
    lanes = [slice(hh * LANES, (hh + 1) * LANES) for hh in range(hps)]
    chains = [(hh, d) for hh in range(hps) for d in range(2)]
    s = dict(q=[], k=[], f=[], b=[], vb=[], lvl=[], scores={})

    def prep():
        for d in range(2):
            fr = lbs[d] + (1.0 - lbs[d]) * _sigmoid(z_refs[d][0, rws[d], :].astype(F32))
            s["q"].append(q_ref[0, rws[d], :].astype(F32))
            s["k"].append(1.0 - fr)
            s["f"].append(jnp.maximum(fr, F_FLOOR))
            s["b"].append(_cumsum_rows(tri_refs[d][...], jnp.log(s["f"][d]) * LOG2E, pieces=2))
            s["vb"].append(i_ref[0, rws[d], :])
            s["lvl"].append(lvl_refs[d][...])

    def diagonal():
        for hh, d in chains:
            ls = lanes[hh]
            p = _dot_nt(s["q"][d][:, ls].astype(BF16), s["k"][d][:, ls].astype(BF16))
            s["scores"][hh, d] = jnp.where(s["lvl"][d] == 0, p, 0.0)

    def level_rows(m):
        w, qrows = [], []
        for d in range(2):
            b, q, k = s["b"][d], s["q"][d], s["k"][d]
            parts, qpart = [], []
            for g0 in range(0, CHUNK, 2 * m):
                lo, hi = slice(g0, g0 + m), slice(g0 + m, g0 + 2 * m)
                if d == 0:
                    bnd = b[g0 + m - 1:g0 + m, :]
                    parts += [k[lo] * jnp.exp2(bnd - b[lo]), q[hi] * jnp.exp2(b[hi] - bnd)]
                    qpart.append(1)
                else:
                    bnd = b[g0 + m:g0 + m + 1, :]
                    parts += [q[lo] * jnp.exp2(b[lo] - bnd), k[hi] * jnp.exp2(bnd - b[hi])]
                    qpart.append(0)
            w.append(jnp.concatenate(parts, axis=0).astype(BF16))
            qsel = [parts[2 * g + qpart[g]] for g in range(len(qpart))]
            qrows.append((qsel[0] if len(qsel) == 1 else jnp.concatenate(qsel, axis=0)).astype(BF16))
        for hh, d in chains:
            ls = lanes[hh]
            p = _dot_nt(qrows[d][:, ls], w[d][:, ls])
            sc, lv, out = s["scores"][hh, d], s["lvl"][d], []
            for g, g0 in enumerate(range(0, CHUNK, 2 * m)):
                lo, hi = slice(g0, g0 + m), slice(g0 + m, g0 + 2 * m)
                qs = hi if d == 0 else lo
                upd = jnp.where(lv[qs] == m, p[g * m:(g + 1) * m], sc[qs])
                out += [sc[lo], upd] if d == 0 else [upd, sc[hi]]
            s["scores"][hh, d] = jnp.concatenate(out, axis=0)

    def level(m):
        if m >= 8:
            return level_rows(m)
        row = lax.broadcasted_iota(jnp.int32, (CHUNK, hps * LANES), 0)
        w = []
        for d in range(2):
            bit = (row & m) != 0
            qside = jnp.logical_not(bit) if d == 1 else bit
            if m == 1:
                e = jnp.where(qside, s["f"][d], 1.0)
            else:
                bm = _boundary_rows(s["b"][d], m, d == 1)
                e = jnp.exp2(jnp.where(qside, s["b"][d] - bm, bm - s["b"][d]))
            w.append((jnp.where(qside, s["q"][d], s["k"][d]) * e).astype(BF16))
        for hh, d in chains:
            wh = w[d][:, lanes[hh]]
            s["scores"][hh, d] = jnp.where(s["lvl"][d] == m, _dot_nt(wh, wh), s["scores"][hh, d])

    def final():
        qe, kdec, eb = [], [], []
        for d in range(2):
            b = s["b"][d]
            b_end = b[0:1, :] if d == 1 else b[CHUNK - 1:CHUNK, :]
            qe.append((s["q"][d] * jnp.exp2(b)).astype(BF16))
            kdec.append((s["k"][d] * jnp.exp2(b_end - b)).astype(BF16))
            eb.append(jnp.exp2(b_end))
        for hh, d in chains:
            ls = lanes[hh]
            st = st_ref[2 * hh + d]
            o = jnp.dot(s["scores"][hh, d].astype(BF16), s["vb"][d][:, ls], preferred_element_type=F32)
            o = o + _dot_nt(qe[d][:, ls], st.astype(BF16))
            st_ref[2 * hh + d] = st * eb[d][:, ls] + _dot_tn(s["vb"][d][:, ls], kdec[d][:, ls])
            acc_refs[d][0, rws[d], ls] = o.astype(acc_refs[d].dtype)

    levels = []
    m = 1
    while m < CHUNK:
        levels.append(functools.partial(level, m))
        m *= 2
    return [prep, diagonal] + levels + [final]


MASKED_EXPONENT = -1e30


def _ssd_tables(gps):
    hps = gps * B_HPG
    e3 = np.zeros((2, B_HEADS // hps, LANES, hps * LANES), np.float32)
    e2 = np.zeros((2, B_HEADS // hps, LANES, hps * B_HEADDIM), np.float32)
    for d in range(2):
        for s in range(B_HEADS // hps):
            for jj in range(hps):
                lane = 16 * d + hps * s + jj
                for piece in range(3):
                    e3[d, s, 32 * piece + lane, jj * LANES:(jj + 1) * LANES] = 1.0
                for piece in range(2):
                    e2[d, s, 32 * piece + lane, jj * B_HEADDIM:(jj + 1) * B_HEADDIM] = 1.0
    return e3, e2


def _ssd_stages(xs_ref, bm_ref, cm_ref, dt_ref, cum_ref, e3_ref, e2_ref, acc_refs, st_ref, rowt_ref, g0, rws):
    gw = B_HPG * B_HEADDIM
    gps = xs_ref.shape[2] // gw
    c = CHUNK
    chains = [(gg, d) for gg in range(gps) for d in range(2)]
    s = dict(colb=[], decay_in=[], gain=[], bmb={}, cmb={}, xsf={}, cbs={}, y_intra={})

    def prep():
        lane = lax.broadcasted_iota(jnp.int32, (c, LANES), 1)
        for d in range(2):
            dt = dt_ref[0, rws[d], :]
            cum = cum_ref[0, rws[d], :]
            hi, mid, lo = _split3(cum)
            cum3 = jnp.where(lane < 32, hi, jnp.where(lane < 64, mid, lo))
            s["colb"].append(jnp.dot(cum3, e3_ref[d, 0], preferred_element_type=F32))
            rowt_ref[d] = (jnp.log(dt) - cum).T

            def expand(vals):
                vhi = vals.astype(BF16)
                vlo = (vals - vhi.astype(F32)).astype(BF16)
                return jnp.dot(jnp.where(lane < 32, vhi, vlo), e2_ref[d, 0], preferred_element_type=F32)

            cum_end = cum[0:1, :] if d == 1 else cum[c - 1:c, :]
            s["decay_in"].append(expand(jnp.exp(cum)))
            s["gain"].append(expand(jnp.exp(jnp.minimum(cum_end - cum, 0.0)) * dt))

    def cb():
        for gg, d in chains:
            s["bmb"][gg, d] = bm_ref[0, rws[d], gg * LANES:(gg + 1) * LANES]
            s["cmb"][gg, d] = cm_ref[0, rws[d], gg * LANES:(gg + 1) * LANES]
            s["xsf"][gg, d] = xs_ref[0, rws[d], gg * gw:(gg + 1) * gw]
            s["cbs"][gg, d] = _dot_nt(s["cmb"][gg, d], s["bmb"][gg, d])

    def intra(gg, d):
        ti = lax.broadcasted_iota(jnp.int32, (c, c), 0)
        si = lax.broadcasted_iota(jnp.int32, (c, c), 1)
        causal = (si >= ti) if d == 1 else (si <= ti)
        half = lax.broadcasted_iota(jnp.int32, (c, LANES), 1) < B_HEADDIM
        ys = []
        for pair in range(B_HPG // 2):
            xp = s["xsf"][gg, d][:, pair * LANES:(pair + 1) * LANES]
            xblk = jnp.concatenate([jnp.where(half, xp, jnp.zeros_like(xp)),
                                    jnp.where(half, jnp.zeros_like(xp), xp)], axis=0)
            mms = []
            for sub in range(2):
                j = 2 * pair + sub
                ln = 16 * d + B_HPG * (g0 + gg) + j
                col = (gg * B_HPG + j) * LANES
                arg = s["colb"][d][:, col:col + LANES] + rowt_ref[d, pl.ds(ln, 1), :]
                lmat = jnp.exp(jnp.where(causal, arg, MASKED_EXPONENT))
                mms.append((s["cbs"][gg, d] * lmat).astype(BF16))
            ys.append(jnp.dot(jnp.concatenate(mms, axis=1), xblk, preferred_element_type=F32))
        s["y_intra"][gg, d] = jnp.concatenate(ys, axis=1)

    def state(gg, d):
        k = 2 * gg + d
        cs = slice(gg * gw, (gg + 1) * gw)
        st = st_ref[k]
        dec = s["decay_in"][d][:, cs]
        y_inter = jnp.dot(s["cmb"][gg, d], st.astype(BF16), preferred_element_type=F32) * dec
        upd = _dot_tn(s["bmb"][gg, d], s["xsf"][gg, d] * s["gain"][d][:, cs].astype(BF16))
        dec_row = dec[0:1, :] if d == 1 else dec[c - 1:c, :]
        st_ref[k] = st * dec_row + upd
        acc_refs[d][0, rws[d], cs] = (s["y_intra"][gg, d] + y_inter).astype(acc_refs[d].dtype)

    return ([prep, cb] + [functools.partial(intra, gg, d) for gg, d in chains]
            + [functools.partial(state, gg, d) for gg, d in chains])


def _interleave(a, b):
    out, ib = [], 0
    for ia, fa in enumerate(a):
        out.append(fa)
        want = (ia + 1) * len(b) // len(a)
        while ib < want:
            out.append(b[ib])
            ib += 1
    return out + b[ib:]


def _scan_kernel(q_ref, ff_ref, fb_ref, i_ref, lbl_ref, trif_ref, trib_ref, lvlf_ref, lvlb_ref,
                 xs_ref, bm_ref, cm_ref, dt_ref, cum_ref, e3_ref, e2_ref,
                 of_ref, ob_ref, yf_ref, yb_ref, sta_ref, stb_ref, rowt_ref, *, layer, n_ctx_chunks):
    n_chunks = q_ref.shape[1] // CHUNK
    gps = xs_ref.shape[2] // (B_HPG * B_HEADDIM)
    g0 = pl.program_id(1) * gps

    def lower_bound(d):
        lg = lbl_ref[d]
        e = jnp.exp(lg - jnp.max(lg, axis=0, keepdims=True))
        sm = e / jnp.sum(e, axis=0, keepdims=True)
        lb = jnp.zeros((1, lg.shape[1]), F32)
        for l2 in range(1, layer + 1):
            lb = lb + sm[l2:l2 + 1, :]
        return lb

    lbs = (lower_bound(0), lower_bound(1))
    sta_ref[...] = jnp.zeros_like(sta_ref)
    stb_ref[...] = jnp.zeros_like(stb_ref)

    def body(i, carry):
        c_b = jnp.where(i < n_ctx_chunks, n_ctx_chunks - 1 - i, n_chunks - 1 - (i - n_ctx_chunks))
        rws = (_chunk_rows(i), _chunk_rows(c_b))
        h = _hgrn2_stages(q_ref, (ff_ref, fb_ref), i_ref, (trif_ref, trib_ref), (lvlf_ref, lvlb_ref),
                          (of_ref, ob_ref), sta_ref, lbs, rws)
        s = _ssd_stages(xs_ref, bm_ref, cm_ref, dt_ref, cum_ref, e3_ref, e2_ref, (yf_ref, yb_ref), stb_ref,
                        rowt_ref, g0, rws)
        for stage in [s[0], h[0]] + _interleave(h[1:], s[1:]):
            stage()
        return carry

    lax.fori_loop(0, n_chunks, body, 0)


SCAN_SPLIT = 2


def _scan(u, dt, cum, lb_logits, layer, lc):
    bsz, t_tot, _ = u.shape
    depth = lb_logits.shape[1]
    hps, gps = A_HEADS // SCAN_SPLIT, B_GROUPS // SCAN_SPLIT
    gw = B_HPG * B_HEADDIM
    awid, wblk, nblk = hps * LANES, gps * gw, gps * LANES
    tri_f = np.tril(np.ones((CHUNK, CHUNK), np.float32))
    lvl_f = _level_table()
    e3, e2 = _ssd_tables(gps)
    ca, cx = COL_A // awid, COL_XBC // wblk
    cbm = (COL_XBC + B_WIDTH) // nblk
    ccm = cbm + B_GROUPS // gps
    aseq = lambda off: pl.BlockSpec((1, t_tot, awid), lambda b, s: (b, 0, ca + off * SCAN_SPLIT + s))
    const = lambda: pl.BlockSpec((CHUNK, CHUNK), lambda b, s: (0, 0))
    lane_seq = lambda: pl.BlockSpec((1, t_tot, LANES), lambda b, s: (b, 0, 0))
    a_out = jax.ShapeDtypeStruct((bsz, t_tot, A_KW), BF16)
    b_out = jax.ShapeDtypeStruct((bsz, t_tot, B_WIDTH), BF16)
    a_spec = pl.BlockSpec((1, t_tot, awid), lambda b, s: (b, 0, s))
    b_spec = pl.BlockSpec((1, t_tot, wblk), lambda b, s: (b, 0, s))
    kern = functools.partial(_scan_kernel, layer=layer, n_ctx_chunks=lc // CHUNK)
    of, ob, yf, yb = pl.pallas_call(
        kern,
        out_shape=(a_out, a_out, b_out, b_out),
        grid=(bsz, SCAN_SPLIT),
        in_specs=[aseq(0), aseq(1), aseq(2), aseq(3),
                  pl.BlockSpec((2, depth, awid), lambda b, s: (0, 0, s)),
                  const(), const(), const(), const(),
                  pl.BlockSpec((1, t_tot, wblk), lambda b, s: (b, 0, cx + s)),
                  pl.BlockSpec((1, t_tot, nblk), lambda b, s: (b, 0, cbm + s)),
                  pl.BlockSpec((1, t_tot, nblk), lambda b, s: (b, 0, ccm + s)),
                  lane_seq(), lane_seq(),
                  pl.BlockSpec((2, 1, LANES, gps * B_HPG * LANES), lambda b, s: (0, s, 0, 0)),
                  pl.BlockSpec((2, 1, LANES, wblk), lambda b, s: (0, s, 0, 0))],
        out_specs=(a_spec, a_spec, b_spec, b_spec),
        scratch_shapes=[pltpu.VMEM((2 * hps, A_DK, A_DK), F32), pltpu.VMEM((2 * gps, B_STATE, gw), F32),
                        pltpu.VMEM((2, LANES, CHUNK), F32)],
        compiler_params=_cparams(("parallel", "parallel")),
        name="scan",
    )(u, u, u, u, lb_logits,
      jnp.asarray(tri_f, BF16), jnp.asarray(tri_f.T, BF16), jnp.asarray(lvl_f), jnp.asarray(lvl_f.T),
      u, u, u, dt, cum, jnp.asarray(e3, BF16), jnp.asarray(e2, BF16))
    return (of, ob), (yf, yb)


def _group_rms(v, width):
    parts = []
    for c0 in range(0, v.shape[1], width):
        p = v[:, c0:c0 + width]
        parts.append(p * lax.rsqrt(jnp.mean(p * p, axis=-1, keepdims=True) + RMS_EPS))
    return jnp.concatenate(parts, axis=1)


def _merge_kernel(*refs, n_x, lc, t_off, alpha):
    x_refs = refs[:n_x]
    (mod_ref, of_ref, ob_ref, g_ref, yf_ref, yb_ref, xs_ref, z_ref, ga_ref, gb_ref,
     anw_ref, dsk_ref, bnw_ref, wpa_ref, wpb_ref, wo_ref, lng_ref, lnb_ref, o_ref) = refs[n_x:]
    t = pl.program_id(1)
    tm = o_ref.shape[1]
    sub = min(tm, lc)
    n_sub = tm // sub
    rs = [slice(k * sub, (k + 1) * sub) for k in range(n_sub)]
    ybf, pab, mrg, out = {}, {}, {}, {}

    def norms(k):
        r = rs[k]
        oa = of_ref[0, r, :].astype(F32) + ob_ref[0, r, :].astype(F32)
        ya = _group_rms(oa, A_DK) * anw_ref[...] * _silu(g_ref[0, r, :].astype(F32))
        yb = yf_ref[0, r, :].astype(F32) + yb_ref[0, r, :].astype(F32) + dsk_ref[...] * xs_ref[0, r, :].astype(F32)
        yb = _group_rms(yb * _silu(z_ref[0, r, :].astype(F32)), B_WIDTH // B_GROUPS) * bnw_ref[...]
        ybf[k] = (ya.astype(BF16), yb.astype(BF16))

    def project(k):
        ya, yb = ybf.pop(k)
        pab[k] = (jnp.dot(ya, wpa_ref[...], preferred_element_type=F32),
                  jnp.dot(yb, wpb_ref[...], preferred_element_type=F32))

    def gates(k):
        r = rs[k]
        pa, pb = pab.pop(k)
        mrg[k] = _sigmoid(ga_ref[0, r, :]) * pa.astype(BF16) + _sigmoid(gb_ref[0, r, :]) * pb.astype(BF16)

    def outproj(k):
        out[k] = jnp.dot(mrg.pop(k), wo_ref[...], preferred_element_type=F32)

    def residual_norm(k):
        r = rs[k]
        tok = (t + t_off) * tm + k * sub + lax.broadcasted_iota(jnp.int32, (sub, 1), 0)
        gate = jnp.where(tok < lc, mod_ref[0, 1, 2:3, :], mod_ref[0, 0, 2:3, :])
        y = alpha * _x_rows(x_refs, t, k * sub, sub) + gate * out.pop(k)
        mu = jnp.mean(y, axis=-1, keepdims=True)
        yc = y - mu
        var = jnp.mean(yc * yc, axis=-1, keepdims=True)
        o_ref[0, r, :] = yc * lax.rsqrt(var + LN_EPS) * lng_ref[...] + lnb_ref[...]

    stages = (norms, project, gates, outproj, residual_norm)
    for step in range(n_sub + len(stages) - 1):
        for s, stage in reversed(list(enumerate(stages))):
            if 0 <= step - s < n_sub:
                stage(step - s)


def _merge(xparts, mod, oa, yb, u, a_norm_w, d_skip, b_norm_w, wpa, wpb, wo, ln_g, ln_b, lc, alpha, latent_only):
    bsz, d = xparts[0].shape[0], xparts[0].shape[2]
    t_tot = sum(p.shape[1] for p in xparts)
    tm = _pick_tile(lc, (256,)) if latent_only else _pick_tile(t_tot, (768, 512, 256))
    t_off = lc // tm if latent_only else 0
    n_t = t_tot // tm - t_off
    seq = lambda w, col=0: pl.BlockSpec((1, tm, w), lambda b, t: (b, t + t_off, col // w))
    full = lambda a: pl.BlockSpec(a.shape, lambda b, t: (0,) * a.ndim, pipeline_mode=pl.Buffered(1))
    x_specs = _x_specs(xparts, tm, lc, t_off)
    kern = functools.partial(_merge_kernel, n_x=len(x_specs), lc=lc, t_off=t_off, alpha=alpha)
    lng = ln_g.reshape(1, d)
    lnb = ln_b.reshape(1, d)
    anw = jnp.tile(a_norm_w.reshape(1, A_DK), (1, A_HEADS))
    dsk = jnp.repeat(d_skip, B_HEADDIM).reshape(1, B_WIDTH)
    bnw = b_norm_w.reshape(1, B_WIDTH)
    return pl.pallas_call(
        kern,
        out_shape=jax.ShapeDtypeStruct((bsz, n_t * tm, d), F32),
        grid=(bsz, n_t),
        in_specs=x_specs + [pl.BlockSpec((1, 2, 3, d), lambda b, t: (b, 0, 0, 0)),
                            seq(A_KW), seq(A_KW), seq(A_KW, COL_A + 4 * A_KW),
                            seq(B_WIDTH), seq(B_WIDTH), seq(B_WIDTH, COL_XBC), seq(B_WIDTH, COL_Z),
                            seq(d, COL_GATE), seq(d, COL_GATE + d),
                            full(anw), full(dsk), full(bnw), full(wpa), full(wpb), full(wo), full(lng), full(lnb)],
        out_specs=pl.BlockSpec((1, tm, d), lambda b, t: (b, t, 0)),
        compiler_params=_cparams(("parallel", "parallel")),
        name="merge",
    )(*_x_operands(xparts, tm, lc), mod, oa[0], oa[1], u, yb[0], yb[1], u, u, u, u,
      anw, dsk, bnw, wpa, wpb, wo, lng, lnb)


def kernel(x, c, ctx, c_ctx, w_mod, b_mod, w_in, a_lb_logits, a_norm_w, b_conv_w, b_conv_b, b_dt_bias, b_a_log,
           b_d, b_norm_w, w_proj_a, w_proj_b, w_out, ln_g, ln_b):
    bsz, seq, d = x.shape
    lc = ctx.shape[1]
    depth = w_mod.shape[0]
    assert 2 * d == COL_Z - COL_GATE and d == 2 * A_KW and d == B_WIDTH and seq % GRID_W == 0
    assert lc % 256 == 0 and (lc + seq) % 256 == 0 and seq % CHUNK == 0
    assert w_in.shape[2] == W_COL_GATE + 2 * d
    alpha = (2 * depth) ** 0.25

    rows = -(-(bsz + 1) // 8) * 8
    cc = jnp.zeros((rows, d), F32).at[:bsz].set(c).at[bsz].set(c_ctx)
    mod_all = _modulation(cc, w_mod, b_mod)
    ml = mod_all[:, :bsz].reshape(depth, bsz, 1, 3, d)
    mc = jnp.broadcast_to(mod_all[:, bsz].reshape(depth, 1, 1, 3, d), (depth, bsz, 1, 3, d))
    mod_all = jnp.concatenate([ml, mc], axis=2)

    w_all = w_in.astype(BF16)
    w_gate = w_all[:, :, W_COL_GATE:]
    w_dt = jnp.tile(w_all[:, :, W_COL_DT:W_COL_GATE], (1, 1, LANES // (2 * B_HEADS)))
    wpa = w_proj_a.astype(BF16)
    wpb = w_proj_b.astype(BF16)
    wo = w_out.astype(BF16)

    xparts = (ctx, x)
    for l in range(depth):
        mod = mod_all[l]
        u, dt, cum = _inproj(xparts, mod, w_all, l, w_gate[l], w_dt[l], b_conv_w[l], b_conv_b[l].reshape(1, -1),
                             b_dt_bias[l], b_a_log[l], lc)
        oa, yb = _scan(u, dt, cum, a_lb_logits, l, lc)
        xparts = (_merge(xparts, mod, oa, yb, u, a_norm_w[l], b_d[l], b_norm_w[l], wpa[l], wpb[l], wo[l],
                         ln_g[l], ln_b[l], lc, alpha, latent_only=(l == depth - 1)),)
    return xparts[0]
```
